```python
import math
import jax, jax.numpy as jnp
from jax import lax
import numpy as np

D_MODEL = 2048
BATCH = 2
SEQ = 8192
DEPTH = 1

CHUNK = 64
N_PREV_CHUNKS = 8
BAND = N_PREV_CHUNKS + 1
REL_CLIP = 128

A_HEADS = 8
A_HEAD_DIM = 128
A_WIDTH = A_HEADS * A_HEAD_DIM

LRU_WIDTH = 512
LRU_BLOCKS = 8
LRU_BLOCK_DIM = LRU_WIDTH // LRU_BLOCKS
CONV_WIDTH = 4
LRU_C = 8.0

N_MEM = 256
C_HEADS = 4
C_HEAD_DIM = 128
C_WIDTH = C_HEADS * C_HEAD_DIM

MIX_WIDTH = A_WIDTH + LRU_WIDTH + C_WIDTH
IN_WIDTH = 3 * A_WIDTH + 2 * LRU_WIDTH + C_WIDTH
N_BRANCHES = 3

PEER_HEADS = 8
N_KEYS = 128
N_EXPERTS = N_KEYS * N_KEYS
PEER_QDIM = 256
PEER_HALF = PEER_QDIM // 2
PEER_TOPK = 16
TOK_BLOCK = 128

EPS = 1e-6
NEG = -1e30

kernel_name = "hybrid_chunk_attn_rglru_memxattn_peer"


def rms_norm(x, g):
    xf = x.astype(jnp.float32)
    y = xf * lax.rsqrt(jnp.mean(xf * xf, axis=-1, keepdims=True) + EPS)
    return (y * g.astype(jnp.float32)).astype(x.dtype)


def chunk_band_attention(q, k, v, rel_bias):
    B, S, H, Dh = q.shape
    nC = S // CHUNK
    qc = q.reshape(B, nC, CHUNK, H, Dh)
    pad = ((0, 0), (N_PREV_CHUNKS, 0), (0, 0), (0, 0), (0, 0))
    kp = jnp.pad(k.reshape(B, nC, CHUNK, H, Dh), pad)
    vp = jnp.pad(v.reshape(B, nC, CHUNK, H, Dh), pad)
    kb = jnp.concatenate([kp[:, j:j + nC] for j in range(BAND)], axis=2)
    vb = jnp.concatenate([vp[:, j:j + nC] for j in range(BAND)], axis=2)
    s = jnp.einsum('bnqhd,bnkhd->bnhqk', qc, kb).astype(jnp.float32) * (Dh ** -0.5)
    qi = np.arange(CHUNK)[:, None]
    km = np.arange(BAND * CHUNK)[None, :]
    dist = N_PREV_CHUNKS * CHUNK + qi - km
    ridx = np.clip(dist, -REL_CLIP, REL_CLIP) + REL_CLIP
    bias = rel_bias[:, ridx].astype(jnp.float32)
    valid = (np.arange(nC)[:, None] + km // CHUNK) >= N_PREV_CHUNKS
    s = jnp.where(valid[None, :, None, None, :], s + bias[None, None], NEG)
    p = jax.nn.softmax(s, axis=-1).astype(v.dtype)
    o = jnp.einsum('bnhqk,bnkhd->bnqhd', p, vb)
    return o.reshape(B, S, H * Dh)


def rg_lru_branch(xl, gl, conv_w, conv_b, wa, ba, wx, bx, lam):
    B, S, W = xl.shape
    xc = lax.conv_general_dilated(
        xl, conv_w.astype(xl.dtype)[:, None, :], window_strides=(1,),
        padding=[(CONV_WIDTH - 1, 0)], dimension_numbers=('NWC', 'WIO', 'NWC'),
        feature_group_count=W) + conv_b
    xg = xc.reshape(B, S, LRU_BLOCKS, LRU_BLOCK_DIM)
    r = jax.nn.sigmoid(jnp.einsum('bsgi,gij->bsgj', xg, wa).reshape(B, S, W) + ba).astype(jnp.float32)
    i = jax.nn.sigmoid(jnp.einsum('bsgi,gij->bsgj', xg, wx).reshape(B, S, W) + bx).astype(jnp.float32)
    log_a = -LRU_C * r * jax.nn.softplus(-lam.astype(jnp.float32))
    a = jnp.exp(log_a)
    mult = jnp.sqrt(jnp.clip(1.0 - jnp.exp(2.0 * log_a), 1e-12, 1.0))
    u = mult * (i * xc.astype(jnp.float32))

    def combine(l, r_):
        a1, b1 = l
        a2, b2 = r_
        return a1 * a2, a2 * b1 + b2

    _, h = lax.associative_scan(combine, (a, u), axis=1)
    return (h * jax.nn.gelu(gl.astype(jnp.float32))).astype(xl.dtype)


def memory_cross_attention(qc, mem, mem_norm_g, w_mem_kv, xq_g, xk_g):
    B, S, _ = qc.shape
    mn = rms_norm(mem, mem_norm_g)
    kv = mn @ w_mem_kv
    k, v = jnp.split(kv, 2, axis=-1)
    q = rms_norm(qc.reshape(B, S, C_HEADS, C_HEAD_DIM), xq_g)
    k = rms_norm(k.reshape(B, N_MEM, C_HEADS, C_HEAD_DIM), xk_g)
    v = v.reshape(B, N_MEM, C_HEADS, C_HEAD_DIM)
    s = jnp.einsum('bshd,bmhd->bhsm', q, k).astype(jnp.float32) * (C_HEAD_DIM ** -0.5)
    p = jax.nn.softmax(s, axis=-1).astype(v.dtype)
    return jnp.einsum('bhsm,bmhd->bshd', p, v).reshape(B, S, C_WIDTH)


def peer(h, w_peer_q, peer_keys, peer_u, peer_v):
    B, S, D = h.shape
    q = (h @ w_peer_q).reshape(B, S, PEER_HEADS, 2, PEER_HALF)
    s = jnp.einsum('bshcd,hckd->bshck', q, peer_keys).astype(jnp.float32)
    sv, si = lax.top_k(s, PEER_TOPK)
    cand = (sv[..., 0, :, None] + sv[..., 1, None, :]).reshape(B, S, PEER_HEADS, PEER_TOPK * PEER_TOPK)
    cidx = (si[..., 0, :, None] * N_KEYS + si[..., 1, None, :]).reshape(B, S, PEER_HEADS, PEER_TOPK * PEER_TOPK)
    fv, fpos = lax.top_k(cand, PEER_TOPK)
    idx = jnp.take_along_axis(cidx, fpos, axis=-1)
    g = jax.nn.softmax(fv, axis=-1).astype(h.dtype)
    nb = (B * S) // TOK_BLOCK
    xb = h.reshape(nb, TOK_BLOCK, D)
    ib = idx.reshape(nb, TOK_BLOCK, PEER_HEADS, PEER_TOPK)
    gb = g.reshape(nb, TOK_BLOCK, PEER_HEADS, PEER_TOPK)

    def block(args):
        xt, it, gt = args
        u = peer_u[it]
        act = jax.nn.gelu(jnp.einsum('thkd,td->thk', u, xt)) * gt
        return jnp.einsum('thk,thkd->td', act, peer_v[it])

    out = lax.map(block, (xb, ib, gb))
    return out.reshape(B, S, D)


def setup_inputs(seed: int = 0) -> dict:
    key = jax.random.key(seed)
    ks = jax.random.split(key, 32)
    f32 = jnp.float32
    D = D_MODEL
    nrm = lambda k, shape, scale: jax.random.normal(k, shape, f32) * scale
    gain = lambda k, shape: 1.0 + 0.02 * jax.random.normal(k, shape, f32)
    a_init = jax.random.uniform(ks[10], (LRU_WIDTH,), f32, 0.9, 0.999) ** (1.0 / LRU_C)
    lru_lambda = jnp.log(a_init) - jnp.log1p(-a_init)
    w_branch = jnp.concatenate([
        nrm(ks[17], (A_WIDTH, D), A_WIDTH ** -0.5),
        nrm(ks[18], (LRU_WIDTH, D), LRU_WIDTH ** -0.5),
        nrm(ks[19], (C_WIDTH, D), C_WIDTH ** -0.5)], axis=0)
    return {
        "x": nrm(ks[0], (BATCH, SEQ, D), 1.0),
        "mem": nrm(ks[1], (BATCH, N_MEM, D), 1.0),
        "norm1_g": gain(ks[2], (D,)),
        "w_in": nrm(ks[3], (D, IN_WIDTH), D ** -0.5),
        "attn_q_norm_g": gain(ks[4], (A_HEAD_DIM,)),
        "attn_k_norm_g": gain(ks[5], (A_HEAD_DIM,)),
        "rel_bias": nrm(ks[6], (A_HEADS, 2 * REL_CLIP + 1), 0.1),
        "conv_w": nrm(ks[7], (CONV_WIDTH, LRU_WIDTH), CONV_WIDTH ** -0.5),
        "conv_b": nrm(ks[8], (LRU_WIDTH,), 0.02),
        "lru_wa": nrm(ks[9], (LRU_BLOCKS, LRU_BLOCK_DIM, LRU_BLOCK_DIM), LRU_BLOCK_DIM ** -0.5),
        "lru_ba": nrm(ks[11], (LRU_WIDTH,), 0.02),
        "lru_wx": nrm(ks[12], (LRU_BLOCKS, LRU_BLOCK_DIM, LRU_BLOCK_DIM), LRU_BLOCK_DIM ** -0.5),
        "lru_bx": nrm(ks[13], (LRU_WIDTH,), 0.02),
        "lru_lambda": lru_lambda,
        "mem_norm_g": gain(ks[14], (D,)),
        "w_mem_kv": nrm(ks[15], (D, 2 * C_WIDTH), D ** -0.5),
        "xq_norm_g": gain(ks[16], (C_HEAD_DIM,)),
        "xk_norm_g": gain(ks[20], (C_HEAD_DIM,)),
        "w_gate": nrm(ks[21], (D, N_BRANCHES * D), D ** -0.5),
        "b_gate": nrm(ks[22], (N_BRANCHES * D,), 0.02),
        "w_branch": w_branch,
        "w_o": nrm(ks[23], (D, D), D ** -0.5),
        "norm2_g": gain(ks[24], (D,)),
        "w_peer_q": nrm(ks[25], (D, PEER_HEADS * PEER_QDIM), D ** -0.5),
        "peer_keys": nrm(ks[26], (PEER_HEADS, 2, N_KEYS, PEER_HALF), PEER_HALF ** -0.5),
        "peer_u": nrm(ks[27], (N_EXPERTS, D), D ** -0.5),
        "peer_v": nrm(ks[28], (N_EXPERTS, D), 0.5),
    }


def reference(x, mem, norm1_g, w_in, attn_q_norm_g, attn_k_norm_g, rel_bias, conv_w, conv_b,
              lru_wa, lru_ba, lru_wx, lru_bx, lru_lambda, mem_norm_g, w_mem_kv, xq_norm_g, xk_norm_g,
              w_gate, b_gate, w_branch, w_o, norm2_g, w_peer_q, peer_keys, peer_u, peer_v):
    B, S, D = x.shape
    for _ in range(DEPTH):
        h = rms_norm(x, norm1_g)
        proj = h @ w_in
        cuts = np.cumsum([A_WIDTH, A_WIDTH, A_WIDTH, LRU_WIDTH, LRU_WIDTH])
        qa, ka, va, xl, gl, qc = jnp.split(proj, cuts, axis=-1)

        qa = rms_norm(qa.reshape(B, S, A_HEADS, A_HEAD_DIM), attn_q_norm_g)
        ka = rms_norm(ka.reshape(B, S, A_HEADS, A_HEAD_DIM), attn_k_norm_g)
        va = va.reshape(B, S, A_HEADS, A_HEAD_DIM)
        o_a = chunk_band_attention(qa, ka, va, rel_bias)

        o_b = rg_lru_branch(xl, gl, conv_w, conv_b, lru_wa, lru_ba, lru_wx, lru_bx, lru_lambda)

        o_c = memory_cross_attention(qc, mem, mem_norm_g, w_mem_kv, xq_norm_g, xk_norm_g)

        y_a = o_a @ w_branch[:A_WIDTH]
        y_b = o_b @ w_branch[A_WIDTH:A_WIDTH + LRU_WIDTH]
        y_c = o_c @ w_branch[A_WIDTH + LRU_WIDTH:]
        g = jax.nn.sigmoid(h @ w_gate + b_gate).reshape(B, S, N_BRANCHES, D)
        merged = g[:, :, 0] * y_a + g[:, :, 1] * y_b + g[:, :, 2] * y_c
        x = x + merged @ w_o

        x = x + peer(rms_norm(x, norm2_g), w_peer_q, peer_keys, peer_u, peer_v)
    return x
```

```python
import functools

import numpy as np
import jax
import jax.numpy as jnp
from jax import lax
from jax.experimental import pallas as pl
from jax.experimental.pallas import tpu as pltpu

F32 = jnp.float32
BF16 = jnp.bfloat16

EPS = 1e-6
NEG = -1e30
LANES = 128
SUBLANES = 8
VMEM_LIMIT = 56 * 1024 * 1024

CHUNK = 64
N_PREV_CHUNKS = 8
REL_CLIP = 128
A_HEADS = 8
HEAD_DIM = 128
C_HEADS = 4
LRU_WIDTH = 512
LRU_C = 8.0
CONV_WIDTH = 4
PEER_HEADS = 8
N_KEYS = 128
PEER_TOPK = 16

ATT_TQ = 256
ATT_TK = ATT_TQ + N_PREV_CHUNKS * CHUNK


def _cparams(sem):
    return pltpu.CompilerParams(dimension_semantics=sem, vmem_limit_bytes=VMEM_LIMIT)


def _gelu(x):
    return 0.5 * x * (1.0 + jnp.tanh(0.7978845608028654 * (x + 0.044715 * (x * x * x))))


def _sigmoid(x):
    return 1.0 / (1.0 + jnp.exp(-x))


def _rmsnorm_kernel(x_ref, g_ref, o_ref):
    x = x_ref[...]
    ms = jnp.mean(x * x, axis=-1, keepdims=True)
    o_ref[...] = (x * lax.rsqrt(ms + EPS) * g_ref[...]).astype(o_ref.dtype)


def _rmsnorm(x, g, tr=512):
    n, d = x.shape
    return pl.pallas_call(
        _rmsnorm_kernel,
        grid=(n // tr,),
        in_specs=[pl.BlockSpec((tr, d), lambda i: (i, 0)),
                  pl.BlockSpec((1, d), lambda i: (0, 0))],
        out_specs=pl.BlockSpec((tr, d), lambda i: (i, 0)),
        out_shape=jax.ShapeDtypeStruct((n, d), BF16),
        compiler_params=_cparams(("parallel",)),
        name="rmsnorm1",
    )(x, g.reshape(1, d))


def _proj_kernel(h_ref, w_ref, g_ref, o_ref, *, plain_lo, plain_hi, tn):
    acc = jnp.dot(h_ref[...], w_ref[...], preferred_element_type=F32)
    j = pl.program_id(1)
    plain = (j >= plain_lo) & (j < plain_hi)

    @pl.when(jnp.logical_not(plain))
    def _():
        for c in range(tn // HEAD_DIM):
            sl = slice(c * HEAD_DIM, (c + 1) * HEAD_DIM)
            blk = acc[:, sl]
            ms = jnp.mean(blk * blk, axis=-1, keepdims=True)
            o_ref[:, sl] = (blk * lax.rsqrt(ms + EPS) * g_ref[:, sl]).astype(o_ref.dtype)

    @pl.when(plain)
    def _():
        o_ref[...] = acc.astype(o_ref.dtype)


def _proj(h, w, gains, plain_cols, out_dtype, name, tm=1024, tn=512):
    n, d = h.shape
    nc = w.shape[1]
    kern = functools.partial(_proj_kernel, plain_lo=plain_cols[0] // tn, plain_hi=plain_cols[1] // tn, tn=tn)
    return pl.pallas_call(
        kern,
        grid=(n // tm, nc // tn),
        in_specs=[pl.BlockSpec((tm, d), lambda i, j: (i, 0)),
                  pl.BlockSpec((d, tn), lambda i, j: (0, j)),
                  pl.BlockSpec((1, tn), lambda i, j: (0, j))],
        out_specs=pl.BlockSpec((tm, tn), lambda i, j: (i, j)),
        out_shape=jax.ShapeDtypeStruct((n, nc), out_dtype),
        compiler_params=_cparams(("parallel", "arbitrary")),
        name=name,
    )(h, w, gains)


def _attn_kernel(q_ref, k0_ref, k1_ref, k2_ref, v0_ref, v1_ref, v2_ref, t_ref, o_ref):
    scale = HEAD_DIM ** -0.5
    nt = (((1,), (1,)), ((), ()))
    for h in range(A_HEADS):
        sl = slice(h * HEAD_DIM, (h + 1) * HEAD_DIM)
        q = q_ref[:, sl]
        s = jnp.concatenate(
            [lax.dot_general(q, kr[:, sl], nt, preferred_element_type=F32)
             for kr in (k0_ref, k1_ref, k2_ref)], axis=1)
        s = s * scale + t_ref[0, h]
        m = jnp.max(s, axis=-1, keepdims=True)
        p = jnp.exp(s - m)
        l = jnp.sum(p, axis=-1, keepdims=True)
        pb = p.astype(BF16)
        o = jnp.dot(pb[:, 0:ATT_TQ], v0_ref[:, sl], preferred_element_type=F32)
        o += jnp.dot(pb[:, ATT_TQ:2 * ATT_TQ], v1_ref[:, sl], preferred_element_type=F32)
        o += jnp.dot(pb[:, 2 * ATT_TQ:], v2_ref[:, sl], preferred_element_type=F32)
        o_ref[:, sl] = (o / l).astype(o_ref.dtype)


def _attn_bias_table(rel_bias):
    qi = np.arange(ATT_TQ)[:, None]
    kj = np.arange(3 * ATT_TQ)[None, :]
    dist = qi - kj + 2 * ATT_TQ
    ridx = np.clip(dist, -REL_CLIP, REL_CLIP) + REL_CLIP
    qch = qi // CHUNK
    kch = kj // CHUNK - (2 * ATT_TQ) // CHUNK
    band = (kch <= qch) & (kch >= qch - N_PREV_CHUNKS)
    bias = rel_bias[:, ridx].astype(F32)
    tabs = []
    for t in range(3):
        valid = band & (kj >= 2 * ATT_TQ - ATT_TQ * t)
        tabs.append(jnp.where(valid[None], bias, NEG))
    return jnp.stack(tabs)


def _band_attention(proj, table, batch, seq, qcol, kcol, vcol):
    n = proj.shape[0]
    aw = A_HEADS * HEAD_DIM
    nq = seq // ATT_TQ

    def qmap(b, t):
        return (b * nq + t, qcol)

    def kvmap(col, back):
        return lambda b, t: (b * nq + jnp.maximum(t - back, 0), col)

    blk = (ATT_TQ, aw)
    return pl.pallas_call(
        _attn_kernel,
        grid=(batch, nq),
        in_specs=[pl.BlockSpec(blk, qmap),
                  pl.BlockSpec(blk, kvmap(kcol, 2)), pl.BlockSpec(blk, kvmap(kcol, 1)),
                  pl.BlockSpec(blk, kvmap(kcol, 0)),
                  pl.BlockSpec(blk, kvmap(vcol, 2)), pl.BlockSpec(blk, kvmap(vcol, 1)),
                  pl.BlockSpec(blk, kvmap(vcol, 0)),
                  pl.BlockSpec((1, A_HEADS, ATT_TQ, 3 * ATT_TQ),
                               lambda b, t: (jnp.minimum(t, 2), 0, 0, 0))],
        out_specs=pl.BlockSpec(blk, lambda b, t: (b * nq + t, 0)),
        out_shape=jax.ShapeDtypeStruct((n, aw), BF16),
        compiler_params=_cparams(("parallel", "arbitrary")),
        name="band_attention",
    )(proj, proj, proj, proj, proj, proj, proj, table)


def _lru_kernel(xg_ref, cw_ref, cb_ref, wbd_ref, bab_ref, lam_ref, o_ref, xpad_ref, hprev_ref, *, tt):
    w = LRU_WIDTH
    t = pl.program_id(1)

    @pl.when(t == 0)
    def _():
        xpad_ref[0:SUBLANES, :] = jnp.zeros((SUBLANES, w), F32)
        hprev_ref[...] = jnp.zeros((1, w), F32)

    xl = xg_ref[:, 0:w]
    gl = xg_ref[:, w:2 * w]
    xpad_ref[SUBLANES:SUBLANES + tt, :] = xl
    xc = cw_ref[3:4, :] * xl + cb_ref[...]
    for k in range(CONV_WIDTH - 1):
        off = SUBLANES - (CONV_WIDTH - 1) + k
        xc = xc + cw_ref[k:k + 1, :] * xpad_ref[off:off + tt, :]
    xpad_ref[0:SUBLANES, :] = xl[tt - SUBLANES:tt, :]

    z = jnp.dot(xc.astype(BF16), wbd_ref[...], preferred_element_type=F32) + bab_ref[...]
    r = _sigmoid(z[:, 0:w])
    i = _sigmoid(z[:, w:2 * w])
    nl = -lam_ref[...]
    softplus = jnp.maximum(nl, 0.0) + jnp.log(1.0 + jnp.exp(-jnp.abs(nl)))
    log_a = -LRU_C * r * softplus
    a = jnp.exp(log_a)
    mult = jnp.sqrt(jnp.clip(1.0 - jnp.exp(2.0 * log_a), 1e-12, 1.0))
    u = mult * (i * xc)

    row = lax.broadcasted_iota(jnp.int32, (tt, w), 0)
    d = 1
    while d < tt:
        keep = row >= d
        a_sh = jnp.where(keep, pltpu.roll(a, d, axis=0), 1.0)
        u_sh = jnp.where(keep, pltpu.roll(u, d, axis=0), 0.0)
        u = a * u_sh + u
        a = a * a_sh
        d *= 2
    h = u + a * hprev_ref[...]
    hprev_ref[...] = h[tt - 1:tt, :]
    o_ref[...] = (h * _gelu(gl)).astype(o_ref.dtype)


def _lru_branch(xg, conv_w, conv_b, wbd, bab, lam, batch, seq, tt=512):
    n = xg.shape[0]
    w = LRU_WIDTH
    nt = seq // tt
    const = lambda b, t: (0, 0)
    return pl.pallas_call(
        functools.partial(_lru_kernel, tt=tt),
        grid=(batch, nt),
        in_specs=[pl.BlockSpec((tt, 2 * w), lambda b, t: (b * nt + t, 0)),
                  pl.BlockSpec((CONV_WIDTH, w), const),
                  pl.BlockSpec((1, w), const),
                  pl.BlockSpec((w, 2 * w), const),
                  pl.BlockSpec((1, 2 * w), const),
                  pl.BlockSpec((1, w), const)],
        out_specs=pl.BlockSpec((tt, w), lambda b, t: (b * nt + t, 0)),
        out_shape=jax.ShapeDtypeStruct((n, w), BF16),
        scratch_shapes=[pltpu.VMEM((tt + SUBLANES, w), F32), pltpu.VMEM((1, w), F32)],
        compiler_params=_cparams(("parallel", "arbitrary")),
        name="rg_lru",
    )(xg, conv_w, conv_b.reshape(1, w), wbd, bab, lam.reshape(1, w))


def _memkv_kernel(mem_ref, g_ref, w_ref, kg_ref, k_ref, v_ref):
    x = mem_ref[0]
    ms = jnp.mean(x * x, axis=-1, keepdims=True)
    mn = (x * lax.rsqrt(ms + EPS) * g_ref[...]).astype(BF16)
    kv = jnp.dot(mn, w_ref[...], preferred_element_type=F32)
    cw = C_HEADS * HEAD_DIM
    for h in range(C_HEADS):
        sl = slice(h * HEAD_DIM, (h + 1) * HEAD_DIM)
        blk = kv[:, sl]
        ms = jnp.mean(blk * blk, axis=-1, keepdims=True)
        k_ref[0, :, sl] = (blk * lax.rsqrt(ms + EPS) * kg_ref[...]).astype(k_ref.dtype)
    v_ref[0] = kv[:, cw:2 * cw].astype(v_ref.dtype)


def _mem_kv(mem, g, w_kv, kg):
    b, m, d = mem.shape
    cw = C_HEADS * HEAD_DIM
    out = jax.ShapeDtypeStruct((b, m, cw), BF16)
    return pl.pallas_call(
        _memkv_kernel,
        grid=(b,),
        in_specs=[pl.BlockSpec((1, m, d), lambda i: (i, 0, 0)),
                  pl.BlockSpec((1, d), lambda i: (0, 0)),
                  pl.BlockSpec((d, 2 * cw), lambda i: (0, 0)),
                  pl.BlockSpec((1, HEAD_DIM), lambda i: (0, 0))],
        out_specs=[pl.BlockSpec((1, m, cw), lambda i: (i, 0, 0)),
                   pl.BlockSpec((1, m, cw), lambda i: (i, 0, 0))],
        out_shape=[out, out],
        compiler_params=_cparams(("parallel",)),
        name="mem_kv",
    )(mem, g.reshape(1, d), w_kv, kg.reshape(1, HEAD_DIM))


def _merge_kernel(h_ref, oa_ref, ob_ref, qc_ref, km_ref, vm_ref, wg0_ref, wg1_ref, wg2_ref, bg_ref,
                  wba_ref, wbb_ref, wbc_ref, o_ref, oc_ref):
    j = pl.program_id(1)

    @pl.when(j == 0)
    def _():
        scale = HEAD_DIM ** -0.5
        nt = (((1,), (1,)), ((), ()))
        for hh in range(C_HEADS):
            sl = slice(hh * HEAD_DIM, (hh + 1) * HEAD_DIM)
            s = lax.dot_general(qc_ref[:, sl], km_ref[0, :, sl], nt, preferred_element_type=F32) * scale
            m = jnp.max(s, axis=-1, keepdims=True)
            p = jnp.exp(s - m)
            l = jnp.sum(p, axis=-1, keepdims=True)
            o = jnp.dot(p.astype(BF16), vm_ref[0, :, sl], preferred_element_type=F32)
            oc_ref[:, sl] = (o / l).astype(oc_ref.dtype)

    h = h_ref[...]
    merged = None
    for br, (o_br, wg_ref, wb_ref) in enumerate(((oa_ref[...], wg0_ref, wba_ref),
                                                 (ob_ref[...], wg1_ref, wbb_ref),
                                                 (oc_ref[...], wg2_ref, wbc_ref))):
        y = jnp.dot(o_br, wb_ref[...], preferred_element_type=F32)
        g = _sigmoid(jnp.dot(h, wg_ref[...], preferred_element_type=F32) + bg_ref[br:br + 1, :])
        merged = g * y if merged is None else merged + g * y
    o_ref[...] = merged.astype(o_ref.dtype)


def _merge(h, o_a, o_b, proj_a, qc_col, k_mem, v_mem, w_gate, b_gate3, w_branch, seq, tm=1024, tn=512):
    n, d = h.shape
    aw = A_HEADS * HEAD_DIM
    cw = C_HEADS * HEAD_DIM
    n_mem = k_mem.shape[1]
    ncol = d // tn
    tiles_per_seq = seq // tm
    return pl.pallas_call(
        _merge_kernel,
        grid=(n // tm, ncol),
        in_specs=[pl.BlockSpec((tm, d), lambda i, j: (i, 0)),
                  pl.BlockSpec((tm, aw), lambda i, j: (i, 0)),
                  pl.BlockSpec((tm, LRU_WIDTH), lambda i, j: (i, 0)),
                  pl.BlockSpec((tm, cw), lambda i, j: (i, qc_col)),
                  pl.BlockSpec((1, n_mem, cw), lambda i, j: (i // tiles_per_seq, 0, 0)),
                  pl.BlockSpec((1, n_mem, cw), lambda i, j: (i // tiles_per_seq, 0, 0)),
                  pl.BlockSpec((d, tn), lambda i, j: (0, j)),
                  pl.BlockSpec((d, tn), lambda i, j: (0, j + ncol)),
                  pl.BlockSpec((d, tn), lambda i, j: (0, j + 2 * ncol)),
                  pl.BlockSpec((3, tn), lambda i, j: (0, j)),
                  pl.BlockSpec((aw, tn), lambda i, j: (0, j)),
                  pl.BlockSpec((LRU_WIDTH, tn), lambda i, j: (aw // LRU_WIDTH, j)),
                  pl.BlockSpec((cw, tn), lambda i, j: ((aw + LRU_WIDTH) // cw, j))],
        out_specs=pl.BlockSpec((tm, tn), lambda i, j: (i, j)),
        out_shape=jax.ShapeDtypeStruct((n, d), BF16),
        scratch_shapes=[pltpu.VMEM((tm, cw), BF16)],
        compiler_params=_cparams(("parallel", "arbitrary")),
        name="gated_merge",
    )(h, o_a, o_b, proj_a, k_mem, v_mem, w_gate, w_gate, w_gate, b_gate3, w_branch, w_branch, w_branch)


def _oproj_kernel(x_ref, m_ref, w_ref, g_ref, x1_ref, h2t_ref):
    x1 = x_ref[...] + jnp.dot(m_ref[...], w_ref[...], preferred_element_type=F32)
    x1_ref[...] = x1
    ms = jnp.mean(x1 * x1, axis=-1, keepdims=True)
    h2 = x1 * lax.rsqrt(ms + EPS) * g_ref[...]
    h2t_ref[...] = h2.T.astype(h2t_ref.dtype)


def _out_proj(x, merged, w_o, g2, tm=512):
    n, d = x.shape
    return pl.pallas_call(
        _oproj_kernel,
        grid=(n // tm,),
        in_specs=[pl.BlockSpec((tm, d), lambda i: (i, 0)),
                  pl.BlockSpec((tm, d), lambda i: (i, 0)),
                  pl.BlockSpec((d, d), lambda i: (0, 0)),
                  pl.BlockSpec((1, d), lambda i: (0, 0))],
        out_specs=[pl.BlockSpec((tm, d), lambda i: (i, 0)),
                   pl.BlockSpec((d, tm), lambda i: (0, i))],
        out_shape=[jax.ShapeDtypeStruct((n, d), F32), jax.ShapeDtypeStruct((d, n), BF16)],
        compiler_params=_cparams(("parallel",)),
        name="out_proj_norm2",
    )(x, merged, w_o, g2.reshape(1, d))


def _route_kernel(ht_ref, wq_ref, kb1_ref, kb2_ref, s1_ref, s2h_ref, st_ref,
                  s2_ref, s2c_ref, top_ref, *, tm):
    nk = N_KEYS * PEER_HEADS
    qt = jnp.dot(wq_ref[...], ht_ref[...], preferred_element_type=F32)
    s1_ref[...] = jnp.dot(kb1_ref[...], qt[0:nk].astype(BF16), preferred_element_type=F32)
    s2_ref[...] = jnp.dot(kb2_ref[...], qt[nk:2 * nk].astype(BF16), preferred_element_type=F32)

    def chunk(tb, carry):
        tsl = pl.ds(pl.multiple_of(tb * LANES, LANES), LANES)
        s2c_ref[...] = s2_ref[:, tsl]
        for h in range(PEER_HEADS):
            s2h_ref[h, :, tsl] = s2c_ref[pl.ds(h, N_KEYS, stride=PEER_HEADS), :]
        for c, s_ref in enumerate((s1_ref, s2_ref)):
            work = s_ref[:, tsl].reshape(N_KEYS, PEER_HEADS, LANES)
            for r in range(PEER_TOPK):
                mx = jnp.max(work, axis=0)
                top_ref[c, r] = mx
                work = jnp.where(work == mx[None], -jnp.inf, work)
        a = [top_ref[0, r] for r in range(PEER_TOPK)]
        b = [top_ref[1, r] for r in range(PEER_TOPK)]
        cands = [a[p] + b[q] for p in range(PEER_TOPK) for q in range(PEER_TOPK)
                 if (p + 1) * (q + 1) <= PEER_TOPK]
        work = list(cands)
        thr = None
        for r in range(PEER_TOPK):
            thr = functools.reduce(jnp.maximum, work)
            if r + 1 < PEER_TOPK:
                work = [jnp.where(wv == thr, -jnp.inf, wv) for wv in work]
        m = a[0] + b[0]
        z = functools.reduce(
            jnp.add, [jnp.where(cv >= thr, jnp.exp(cv - m), 0.0) for cv in cands])
        st_ref[0, :, tsl] = thr
        st_ref[1, :, tsl] = a[0]
        st_ref[2, :, tsl] = b[0]
        st_ref[3, :, tsl] = 1.0 / z
        return carry

    lax.fori_loop(0, tm // LANES, chunk, 0)


def _peer_route(h2t, wq_t, kb1, kb2, tm=512):
    d, n = h2t.shape
    nk = N_KEYS * PEER_HEADS
    s_shape = jax.ShapeDtypeStruct((nk, n), F32)
    return pl.pallas_call(
        functools.partial(_route_kernel, tm=tm),
        grid=(n // tm,),
        in_specs=[pl.BlockSpec((d, tm), lambda i: (0, i)),
                  pl.BlockSpec((2 * nk, d), lambda i: (0, 0)),
                  pl.BlockSpec((nk, nk), lambda i: (0, 0)),
                  pl.BlockSpec((nk, nk), lambda i: (0, 0))],
        out_specs=[pl.BlockSpec((nk, tm), lambda i: (0, i)),
                   pl.BlockSpec((PEER_HEADS, N_KEYS, tm), lambda i: (0, 0, i)),
                   pl.BlockSpec((4, PEER_HEADS, tm), lambda i: (0, 0, i))],
        out_shape=[s_shape, jax.ShapeDtypeStruct((PEER_HEADS, N_KEYS, n), F32),
                   jax.ShapeDtypeStruct((4, PEER_HEADS, n), F32)],
        scratch_shapes=[pltpu.VMEM((nk, tm), F32),
                        pltpu.VMEM((nk, LANES), F32),
                        pltpu.VMEM((2, PEER_TOPK, PEER_HEADS, LANES), F32)],
        compiler_params=_cparams(("parallel",)),
        name="peer_route",
    )(h2t, wq_t, kb1, kb2)


def _peer_kernel(ht_ref, u_ref, vt_ref, s1_ref, s2h_ref, st_ref, acc_ref,
                 e2_ref, c1_ref, z_ref, w_ref, *, tm, te):
    e = pl.program_id(1)
    ni = te // N_KEYS

    @pl.when(e == 0)
    def _():
        acc_ref[...] = jnp.zeros(acc_ref.shape, F32)
        for h in range(PEER_HEADS):
            e2_ref[h] = jnp.exp(s2h_ref[h] - st_ref[2, h:h + 1, :])
        s1 = s1_ref[...].reshape(N_KEYS, PEER_HEADS, tm)
        c1 = jnp.exp(s1 - st_ref[1][None]) * st_ref[3][None]
        c1_ref[...] = c1.reshape(N_KEYS * PEER_HEADS, tm)

    z_ref[...] = jnp.dot(u_ref[...], ht_ref[...], preferred_element_type=F32)

    def chunk(tb, carry):
        tsl = pl.ds(pl.multiple_of(tb * LANES, LANES), LANES)
        thr = st_ref[0, :, tsl]
        for i in range(ni):
            row0 = pl.multiple_of((e * ni + i) * PEER_HEADS, PEER_HEADS)
            s1 = s1_ref[pl.ds(row0, PEER_HEADS), tsl]
            c1 = c1_ref[pl.ds(row0, PEER_HEADS), tsl]
            gate = jnp.zeros((N_KEYS, LANES), F32)
            for h in range(PEER_HEADS):
                score = s2h_ref[h, :, tsl] + s1[h:h + 1, :]
                wgt = e2_ref[h, :, tsl] * c1[h:h + 1, :]
                gate = gate + jnp.where(score >= thr[h:h + 1, :], wgt, 0.0)
            rows = slice(i * N_KEYS, (i + 1) * N_KEYS)
            w_ref[rows, tsl] = (_gelu(z_ref[rows, tsl]) * gate).astype(w_ref.dtype)
        return carry

    lax.fori_loop(0, tm // LANES, chunk, 0)
    acc_ref[...] += jnp.dot(vt_ref[...], w_ref[...], preferred_element_type=F32)


def _peer_dense(h2t, u, vt, s1, s2h, st, tm=512, te=1024):
    d, n = h2t.shape
    ne = u.shape[0]
    nk = N_KEYS * PEER_HEADS
    return pl.pallas_call(
        functools.partial(_peer_kernel, tm=tm, te=te),
        grid=(n // tm, ne // te),
        in_specs=[pl.BlockSpec((d, tm), lambda i, e: (0, i)),
                  pl.BlockSpec((te, d), lambda i, e: (e, 0)),
                  pl.BlockSpec((d, te), lambda i, e: (0, e)),
                  pl.BlockSpec((nk, tm), lambda i, e: (0, i)),
                  pl.BlockSpec((PEER_HEADS, N_KEYS, tm), lambda i, e: (0, 0, i)),
                  pl.BlockSpec((4, PEER_HEADS, tm), lambda i, e: (0, 0, i))],
        out_specs=pl.BlockSpec((d, tm), lambda i, e: (0, i)),
        out_shape=jax.ShapeDtypeStruct((d, n), F32),
        scratch_shapes=[pltpu.VMEM((PEER_HEADS, N_KEYS, tm), F32),
                        pltpu.VMEM((nk, tm), F32),
                        pltpu.VMEM((te, tm), F32),
                        pltpu.VMEM((te, tm), BF16)],
        compiler_params=_cparams(("parallel", "arbitrary")),
        name="peer_dense",
    )(h2t, u, vt, s1, s2h, st)


def _final_kernel(x1_ref, pt_ref, o_ref):
    o_ref[...] = x1_ref[...] + pt_ref[...].T


def _final_add(x1, peer_t, tm=512):
    n, d = x1.shape
    return pl.pallas_call(
        _final_kernel,
        grid=(n // tm,),
        in_specs=[pl.BlockSpec((tm, d), lambda i: (i, 0)),
                  pl.BlockSpec((d, tm), lambda i: (0, i))],
        out_specs=pl.BlockSpec((tm, d), lambda i: (i, 0)),
        out_shape=jax.ShapeDtypeStruct((n, d), F32),
        compiler_params=_cparams(("parallel",)),
        name="final_residual",
    )(x1, peer_t)


def _block_diag(w):
    g, a, b = w.shape
    eye = jnp.eye(g, dtype=w.dtype)
    return jnp.einsum("gab,gh->gahb", w, eye).reshape(g * a, g * b)


def _key_matrix(keys_c):
    h, k, dh = keys_c.shape
    eye = jnp.eye(h, dtype=keys_c.dtype)
    return jnp.einsum("hkd,hg->khgd", keys_c, eye).reshape(k * h, h * dh)


def kernel(x, mem, norm1_g, w_in, attn_q_norm_g, attn_k_norm_g, rel_bias, conv_w, conv_b, lru_wa, lru_ba, lru_wx, lru_bx, lru_lambda, mem_norm_g, w_mem_kv, xq_norm_g, xk_norm_g, w_gate, b_gate, w_branch, w_o, norm2_g, w_peer_q, peer_keys, peer_u, peer_v):
    batch, seq, d = x.shape
    n = batch * seq
    aw = A_HEADS * HEAD_DIM
    cw = C_HEADS * HEAD_DIM
    lw = LRU_WIDTH
    x2 = x.reshape(n, d)

    cuts = np.cumsum([aw, aw, aw, lw, lw])
    wq, wk, wv, wxl, wgl, wqc = jnp.split(w_in, cuts, axis=1)
    w_a = jnp.concatenate([wq, wk, wv, wqc], axis=1).astype(BF16)
    w_b = jnp.concatenate([wxl, wgl], axis=1).astype(BF16)
    gains_a = jnp.concatenate([jnp.tile(attn_q_norm_g, A_HEADS), jnp.tile(attn_k_norm_g, A_HEADS),
                               jnp.ones((aw,), F32), jnp.tile(xq_norm_g, C_HEADS)]).reshape(1, -1)
    wbd = jnp.concatenate([_block_diag(lru_wa), _block_diag(lru_wx)], axis=1).astype(BF16)
    bab = jnp.concatenate([lru_ba, lru_bx]).reshape(1, 2 * lw)
    table = _attn_bias_table(rel_bias)
    hd = PEER_HEADS * N_KEYS
    wq_t = w_peer_q.reshape(d, PEER_HEADS, 2, HEAD_DIM).transpose(2, 1, 3, 0).reshape(2 * hd, d).astype(BF16)
    kb1 = _key_matrix(peer_keys[:, 0]).astype(BF16)
    kb2 = _key_matrix(peer_keys[:, 1]).astype(BF16)
    u_b = peer_u.astype(BF16)
    vt_b = peer_v.T.astype(BF16)

    h = _rmsnorm(x2, norm1_g)
    proj_a = _proj(h, w_a, gains_a, (2 * aw, 3 * aw), BF16, "in_proj_qkv")
    xg = _proj(h, w_b, jnp.ones((1, 2 * lw), F32), (0, 2 * lw), F32, "in_proj_lru")
    o_a = _band_attention(proj_a, table, batch, seq, qcol=0, kcol=1, vcol=2)
    o_b = _lru_branch(xg, conv_w, conv_b, wbd, bab, lru_lambda, batch, seq)
    k_mem, v_mem = _mem_kv(mem, mem_norm_g, w_mem_kv.astype(BF16), xk_norm_g)
    merged = _merge(h, o_a, o_b, proj_a, (3 * aw) // cw, k_mem, v_mem, w_gate.astype(BF16),
                    b_gate.reshape(3, d), w_branch.astype(BF16), seq)
    x1, h2t = _out_proj(x2, merged, w_o.astype(BF16), norm2_g)

    s1, s2h, st = _peer_route(h2t, wq_t, kb1, kb2)
    peer_t = _peer_dense(h2t, u_b, vt_b, s1, s2h, st)
    out = _final_add(x1, peer_t)
    return out.reshape(batch, seq, d)
```

```python
import functools

import numpy as np
import jax
import jax.numpy as jnp
from jax import lax
from jax.experimental import pallas as pl
from jax.experimental.pallas import tpu as pltpu

F32 = jnp.float32
BF16 = jnp.bfloat16

EPS = 1e-6
NEG = -1e30
LANES = 128
SUBLANES = 8
VMEM_LIMIT = 56 * 1024 * 1024

CHUNK = 64
N_PREV_CHUNKS = 8
REL_CLIP = 128
A_HEADS = 8
HEAD_DIM = 128
C_HEADS = 4
LRU_WIDTH = 512
LRU_C = 8.0
CONV_WIDTH = 4
PEER_HEADS = 8
N_KEYS = 128
PEER_TOPK = 16

ROUTE_LANES = 256
ATT_TQ = 256
ATT_TK = ATT_TQ + N_PREV_CHUNKS * CHUNK


def _cparams(sem):
    return pltpu.CompilerParams(dimension_semantics=sem, vmem_limit_bytes=VMEM_LIMIT)


def _gelu(x):
    return 0.5 * x * (1.0 + jnp.tanh(0.7978845608028654 * (x + 0.044715 * (x * x * x))))


_GELU_K1 = -2.0 * 0.7978845608028654 * 1.4426950408889634
_GELU_K2 = _GELU_K1 * 0.044715


def _gelu_sigmoid_form(x):
    return x / (1.0 + jnp.exp2(x * (_GELU_K1 + _GELU_K2 * (x * x))))


def _sigmoid(x):
    return 1.0 / (1.0 + jnp.exp(-x))


def _rmsnorm_kernel(x_ref, g_ref, o_ref):
    x = x_ref[...]
    ms = jnp.mean(x * x, axis=-1, keepdims=True)
    o_ref[...] = (x * lax.rsqrt(ms + EPS) * g_ref[...]).astype(o_ref.dtype)


def _rmsnorm(x, g, tr=512):
    n, d = x.shape
    return pl.pallas_call(
        _rmsnorm_kernel,
        grid=(n // tr,),
        in_specs=[pl.BlockSpec((tr, d), lambda i: (i, 0)),
                  pl.BlockSpec((1, d), lambda i: (0, 0))],
        out_specs=pl.BlockSpec((tr, d), lambda i: (i, 0)),
        out_shape=jax.ShapeDtypeStruct((n, d), BF16),
        compiler_params=_cparams(("parallel",)),
        name="rmsnorm1",
    )(x, g.reshape(1, d))


def _proj_kernel(h_ref, w_ref, g_ref, o_ref, *, plain_lo, plain_hi, tn):
    acc = jnp.dot(h_ref[...], w_ref[...], preferred_element_type=F32)
    j = pl.program_id(1)
    plain = (j >= plain_lo) & (j < plain_hi)

    @pl.when(jnp.logical_not(plain))
    def _():
        for c in range(tn // HEAD_DIM):
            sl = slice(c * HEAD_DIM, (c + 1) * HEAD_DIM)
            blk = acc[:, sl]
            ms = jnp.mean(blk * blk, axis=-1, keepdims=True)
            o_ref[:, sl] = (blk * lax.rsqrt(ms + EPS) * g_ref[:, sl]).astype(o_ref.dtype)

    @pl.when(plain)
    def _():
        o_ref[...] = acc.astype(o_ref.dtype)


def _proj(h, w, gains, plain_cols, out_dtype, name, tm=1024, tn=512):
    n, d = h.shape
    nc = w.shape[1]
    kern = functools.partial(_proj_kernel, plain_lo=plain_cols[0] // tn, plain_hi=plain_cols[1] // tn, tn=tn)
    return pl.pallas_call(
        kern,
        grid=(n // tm, nc // tn),
        in_specs=[pl.BlockSpec((tm, d), lambda i, j: (i, 0)),
                  pl.BlockSpec((d, tn), lambda i, j: (0, j)),
                  pl.BlockSpec((1, tn), lambda i, j: (0, j))],
        out_specs=pl.BlockSpec((tm, tn), lambda i, j: (i, j)),
        out_shape=jax.ShapeDtypeStruct((n, nc), out_dtype),
        compiler_params=_cparams(("parallel", "arbitrary")),
        name=name,
    )(h, w, gains)


def _attn_kernel(q_ref, k0_ref, k1_ref, k2_ref, v0_ref, v1_ref, v2_ref, t_ref, o_ref):
    scale = HEAD_DIM ** -0.5
    nt = (((1,), (1,)), ((), ()))
    for h in range(A_HEADS):
        sl = slice(h * HEAD_DIM, (h + 1) * HEAD_DIM)
        q = q_ref[:, sl]
        s = jnp.concatenate(
            [lax.dot_general(q, kr[:, sl], nt, preferred_element_type=F32)
             for kr in (k0_ref, k1_ref, k2_ref)], axis=1)
        s = s * scale + t_ref[0, h]
        m = jnp.max(s, axis=-1, keepdims=True)
        p = jnp.exp(s - m)
        l = jnp.sum(p, axis=-1, keepdims=True)
        pb = p.astype(BF16)
        o = jnp.dot(pb[:, 0:ATT_TQ], v0_ref[:, sl], preferred_element_type=F32)
        o += jnp.dot(pb[:, ATT_TQ:2 * ATT_TQ], v1_ref[:, sl], preferred_element_type=F32)
        o += jnp.dot(pb[:, 2 * ATT_TQ:], v2_ref[:, sl], preferred_element_type=F32)
        o_ref[:, sl] = (o / l).astype(o_ref.dtype)


def _attn_bias_table(rel_bias):
    nh = rel_bias.shape[0]
    tq, tk = ATT_TQ, 3 * ATT_TQ
    qi = np.arange(tq)[:, None]
    kj = np.arange(tk)[None, :]
    qch = qi // CHUNK
    kch = kj // CHUNK - (2 * tq) // CHUNK
    band = (kch <= qch) & (kch >= qch - N_PREV_CHUNKS)
    n_far = 3 * tq - 1 - REL_CLIP + 1
    n_near = tq + tk - 1 - n_far - (2 * REL_CLIP - 1)
    g = jnp.concatenate([jnp.broadcast_to(rel_bias[:, 2 * REL_CLIP:], (nh, n_far)),
                         rel_bias[:, 1:2 * REL_CLIP][:, ::-1],
                         jnp.broadcast_to(rel_bias[:, :1], (nh, n_near))], axis=1).astype(F32)
    ln = tq + tk
    gp = jnp.pad(g, ((0, 0), (0, 1)))
    m = jnp.tile(gp, (1, tq))[:, :tq * (ln - 1)].reshape(nh, tq, ln - 1)
    bias = m[:, :, tq - 1:tq - 1 + tk]
    tabs = []
    for t in range(3):
        valid = band & (kj >= 2 * ATT_TQ - ATT_TQ * t)
        tabs.append(jnp.where(valid[None], bias, NEG))
    return jnp.stack(tabs)


def _band_attention(proj, table, batch, seq, qcol, kcol, vcol):
    n = proj.shape[0]
    aw = A_HEADS * HEAD_DIM
    nq = seq // ATT_TQ

    def qmap(b, t):
        return (b * nq + t, qcol)

    def kvmap(col, back):
        return lambda b, t: (b * nq + jnp.maximum(t - back, 0), col)

    blk = (ATT_TQ, aw)
    return pl.pallas_call(
        _attn_kernel,
        grid=(batch, nq),
        in_specs=[pl.BlockSpec(blk, qmap),
                  pl.BlockSpec(blk, kvmap(kcol, 2)), pl.BlockSpec(blk, kvmap(kcol, 1)),
                  pl.BlockSpec(blk, kvmap(kcol, 0)),
                  pl.BlockSpec(blk, kvmap(vcol, 2)), pl.BlockSpec(blk, kvmap(vcol, 1)),
                  pl.BlockSpec(blk, kvmap(vcol, 0)),
                  pl.BlockSpec((1, A_HEADS, ATT_TQ, 3 * ATT_TQ),
                               lambda b, t: (jnp.minimum(t, 2), 0, 0, 0))],
        out_specs=pl.BlockSpec(blk, lambda b, t: (b * nq + t, 0)),
        out_shape=jax.ShapeDtypeStruct((n, aw), BF16),
        compiler_params=_cparams(("parallel", "arbitrary")),
        name="band_attention",
    )(proj, proj, proj, proj, proj, proj, proj, table)


def _lru_kernel(xg_ref, cw_ref, cb_ref, wbd_ref, bab_ref, lam_ref, o_ref, xpad_ref, hprev_ref, *, tt):
    w = LRU_WIDTH
    t = pl.program_id(1)

    @pl.when(t == 0)
    def _():
        xpad_ref[0:SUBLANES, :] = jnp.zeros((SUBLANES, w), F32)
        hprev_ref[...] = jnp.zeros((1, w), F32)

    xl = xg_ref[:, 0:w]
    gl = xg_ref[:, w:2 * w]
    xpad_ref[SUBLANES:SUBLANES + tt, :] = xl
    xc = cw_ref[3:4, :] * xl + cb_ref[...]
    for k in range(CONV_WIDTH - 1):
        off = SUBLANES - (CONV_WIDTH - 1) + k
        xc = xc + cw_ref[k:k + 1, :] * xpad_ref[off:off + tt, :]
    xpad_ref[0:SUBLANES, :] = xl[tt - SUBLANES:tt, :]

    z = jnp.dot(xc.astype(BF16), wbd_ref[...], preferred_element_type=F32) + bab_ref[...]
    r = _sigmoid(z[:, 0:w])
    i = _sigmoid(z[:, w:2 * w])
    nl = -lam_ref[...]
    softplus = jnp.maximum(nl, 0.0) + jnp.log(1.0 + jnp.exp(-jnp.abs(nl)))
    log_a = -LRU_C * r * softplus
    a = jnp.exp(log_a)
    mult = jnp.sqrt(jnp.clip(1.0 - jnp.exp(2.0 * log_a), 1e-12, 1.0))
    u = mult * (i * xc)

    row = lax.broadcasted_iota(jnp.int32, (tt, w), 0)
    d = 1
    while d < tt:
        keep = row >= d
        a_sh = jnp.where(keep, pltpu.roll(a, d, axis=0), 1.0)
        u_sh = jnp.where(keep, pltpu.roll(u, d, axis=0), 0.0)
        u = a * u_sh + u
        a = a * a_sh
        d *= 2
    h = u + a * hprev_ref[...]
    hprev_ref[...] = h[tt - 1:tt, :]
    o_ref[...] = (h * _gelu(gl)).astype(o_ref.dtype)


def _lru_branch(xg, conv_w, conv_b, wbd, bab, lam, batch, seq, tt=512):
    n = xg.shape[0]
    w = LRU_WIDTH
    nt = seq // tt
    const = lambda b, t: (0, 0)
    return pl.pallas_call(
        functools.partial(_lru_kernel, tt=tt),
        grid=(batch, nt),
        in_specs=[pl.BlockSpec((tt, 2 * w), lambda b, t: (b * nt + t, 0)),
                  pl.BlockSpec((CONV_WIDTH, w), const),
                  pl.BlockSpec((1, w), const),
                  pl.BlockSpec((w, 2 * w), const),
                  pl.BlockSpec((1, 2 * w), const),
                  pl.BlockSpec((1, w), const)],
        out_specs=pl.BlockSpec((tt, w), lambda b, t: (b * nt + t, 0)),
        out_shape=jax.ShapeDtypeStruct((n, w), BF16),
        scratch_shapes=[pltpu.VMEM((tt + SUBLANES, w), F32), pltpu.VMEM((1, w), F32)],
        compiler_params=_cparams(("parallel", "arbitrary")),
        name="rg_lru",
    )(xg, conv_w, conv_b.reshape(1, w), wbd, bab, lam.reshape(1, w))


def _memkv_kernel(mem_ref, g_ref, w_ref, kg_ref, k_ref, v_ref):
    x = mem_ref[0]
    ms = jnp.mean(x * x, axis=-1, keepdims=True)
    mn = (x * lax.rsqrt(ms + EPS) * g_ref[...]).astype(BF16)
    kv = jnp.dot(mn, w_ref[...], preferred_element_type=F32)
    cw = C_HEADS * HEAD_DIM
    for h in range(C_HEADS):
        sl = slice(h * HEAD_DIM, (h + 1) * HEAD_DIM)
        blk = kv[:, sl]
        ms = jnp.mean(blk * blk, axis=-1, keepdims=True)
        k_ref[0, :, sl] = (blk * lax.rsqrt(ms + EPS) * kg_ref[...]).astype(k_ref.dtype)
    v_ref[0] = kv[:, cw:2 * cw].astype(v_ref.dtype)


def _mem_kv(mem, g, w_kv, kg):
    b, m, d = mem.shape
    cw = C_HEADS * HEAD_DIM
    out = jax.ShapeDtypeStruct((b, m, cw), BF16)
    return pl.pallas_call(
        _memkv_kernel,
        grid=(b,),
        in_specs=[pl.BlockSpec((1, m, d), lambda i: (i, 0, 0)),
                  pl.BlockSpec((1, d), lambda i: (0, 0)),
                  pl.BlockSpec((d, 2 * cw), lambda i: (0, 0)),
                  pl.BlockSpec((1, HEAD_DIM), lambda i: (0, 0))],
        out_specs=[pl.BlockSpec((1, m, cw), lambda i: (i, 0, 0)),
                   pl.BlockSpec((1, m, cw), lambda i: (i, 0, 0))],
        out_shape=[out, out],
        compiler_params=_cparams(("parallel",)),
        name="mem_kv",
    )(mem, g.reshape(1, d), w_kv, kg.reshape(1, HEAD_DIM))


def _merge_kernel(h_ref, oa_ref, ob_ref, qc_ref, km_ref, vm_ref, wg0_ref, wg1_ref, wg2_ref, bg_ref,
                  wba_ref, wbb_ref, wbc_ref, o_ref, oc_ref):
    j = pl.program_id(1)

    @pl.when(j == 0)
    def _():
        scale = HEAD_DIM ** -0.5
        nt = (((1,), (1,)), ((), ()))
        for hh in range(C_HEADS):
            sl = slice(hh * HEAD_DIM, (hh + 1) * HEAD_DIM)
            s = lax.dot_general(qc_ref[:, sl], km_ref[0, :, sl], nt, preferred_element_type=F32) * scale
            m = jnp.max(s, axis=-1, keepdims=True)
            p = jnp.exp(s - m)
            l = jnp.sum(p, axis=-1, keepdims=True)
            o = jnp.dot(p.astype(BF16), vm_ref[0, :, sl], preferred_element_type=F32)
            oc_ref[:, sl] = (o / l).astype(oc_ref.dtype)

    h = h_ref[...]
    merged = None
    for br, (o_br, wg_ref, wb_ref) in enumerate(((oa_ref[...], wg0_ref, wba_ref),
                                                 (ob_ref[...], wg1_ref, wbb_ref),
                                                 (oc_ref[...], wg2_ref, wbc_ref))):
        y = jnp.dot(o_br, wb_ref[...], preferred_element_type=F32)
        g = _sigmoid(jnp.dot(h, wg_ref[...], preferred_element_type=F32) + bg_ref[br:br + 1, :])
        merged = g * y if merged is None else merged + g * y
    o_ref[...] = merged.astype(o_ref.dtype)


def _merge(h, o_a, o_b, proj_a, qc_col, k_mem, v_mem, w_gate, b_gate3, w_branch, seq, tm=1024, tn=512):
    n, d = h.shape
    aw = A_HEADS * HEAD_DIM
    cw = C_HEADS * HEAD_DIM
    n_mem = k_mem.shape[1]
    ncol = d // tn
    tiles_per_seq = seq // tm
    return pl.pallas_call(
        _merge_kernel,
        grid=(n // tm, ncol),
        in_specs=[pl.BlockSpec((tm, d), lambda i, j: (i, 0)),
                  pl.BlockSpec((tm, aw), lambda i, j: (i, 0)),
                  pl.BlockSpec((tm, LRU_WIDTH), lambda i, j: (i, 0)),
                  pl.BlockSpec((tm, cw), lambda i, j: (i, qc_col)),
                  pl.BlockSpec((1, n_mem, cw), lambda i, j: (i // tiles_per_seq, 0, 0)),
                  pl.BlockSpec((1, n_mem, cw), lambda i, j: (i // tiles_per_seq, 0, 0)),
                  pl.BlockSpec((d, tn), lambda i, j: (0, j)),
                  pl.BlockSpec((d, tn), lambda i, j: (0, j + ncol)),
                  pl.BlockSpec((d, tn), lambda i, j: (0, j + 2 * ncol)),
                  pl.BlockSpec((3, tn), lambda i, j: (0, j)),
                  pl.BlockSpec((aw, tn), lambda i, j: (0, j)),
                  pl.BlockSpec((LRU_WIDTH, tn), lambda i, j: (aw // LRU_WIDTH, j)),
                  pl.BlockSpec((cw, tn), lambda i, j: ((aw + LRU_WIDTH) // cw, j))],
        out_specs=pl.BlockSpec((tm, tn), lambda i, j: (i, j)),
        out_shape=jax.ShapeDtypeStruct((n, d), BF16),
        scratch_shapes=[pltpu.VMEM((tm, cw), BF16)],
        compiler_params=_cparams(("parallel", "arbitrary")),
        name="gated_merge",
    )(h, o_a, o_b, proj_a, k_mem, v_mem, w_gate, w_gate, w_gate, b_gate3, w_branch, w_branch, w_branch)


def _oproj_kernel(x_ref, m_ref, w_ref, g_ref, x1_ref, h2t_ref):
    x1 = x_ref[...] + jnp.dot(m_ref[...], w_ref[...], preferred_element_type=F32)
    x1_ref[...] = x1
    ms = jnp.mean(x1 * x1, axis=-1, keepdims=True)
    h2 = x1 * lax.rsqrt(ms + EPS) * g_ref[...]
    h2t_ref[...] = h2.T.astype(h2t_ref.dtype)


def _out_proj(x, merged, w_o, g2, tm=512):
    n, d = x.shape
    return pl.pallas_call(
        _oproj_kernel,
        grid=(n // tm,),
        in_specs=[pl.BlockSpec((tm, d), lambda i: (i, 0)),
                  pl.BlockSpec((tm, d), lambda i: (i, 0)),
                  pl.BlockSpec((d, d), lambda i: (0, 0)),
                  pl.BlockSpec((1, d), lambda i: (0, 0))],
        out_specs=[pl.BlockSpec((tm, d), lambda i: (i, 0)),
                   pl.BlockSpec((d, tm), lambda i: (0, i))],
        out_shape=[jax.ShapeDtypeStruct((n, d), F32), jax.ShapeDtypeStruct((d, n), BF16)],
        compiler_params=_cparams(("parallel",)),
        name="out_proj_norm2",
    )(x, merged, w_o, g2.reshape(1, d))


def _cmpx(v, i, j):
    a, b = v[i], v[j]
    if b is None:
        return
    if a is None:
        v[i], v[j] = b, None
        return
    v[i], v[j] = jnp.maximum(a, b), jnp.minimum(a, b)


def _bitonic_merge_desc(v, lo, n):
    j = n // 2
    while j >= 1:
        for i in range(lo, lo + n):
            if (i - lo) & j == 0:
                _cmpx(v, i, i + j)
        j //= 2


def _bitonic_sort_desc(v):
    n = len(v)
    k = 2
    while k <= n:
        j = k // 2
        while j >= 1:
            for i in range(n):
                l = i ^ j
                if l > i:
                    if i & k == 0 or k == n:
                        _cmpx(v, i, l)
                    else:
                        _cmpx(v, l, i)
            j //= 2
        k *= 2


def _top17_desc(x):
    k = PEER_TOPK
    groups = []
    for g in range(x.shape[0] // k):
        run = [x[g * k + r] for r in range(k)]
        _bitonic_sort_desc(run)
        groups.append(run)
    dropped = None
    while len(groups) > 1:
        merged = []
        for g in range(0, len(groups), 2):
            p, q = groups[g], groups[g + 1]
            hi = [jnp.maximum(p[r], q[k - 1 - r]) for r in range(k)]
            lo = functools.reduce(jnp.maximum, [jnp.minimum(p[r], q[k - 1 - r]) for r in range(k)])
            dropped = lo if dropped is None else jnp.maximum(dropped, lo)
            _bitonic_merge_desc(hi, 0, k)
            merged.append(hi)
        groups = merged
    return groups[0] + [dropped]


def _route_kernel(ht_ref, wq_ref, kb1_ref, kb2_ref, s1_ref, s2h_ref, st_ref,
                  s2_ref, s2c_ref, *, tm):
    nk = N_KEYS * PEER_HEADS
    qt = jnp.dot(wq_ref[...], ht_ref[...], preferred_element_type=F32)
    s1_ref[...] = jnp.dot(kb1_ref[...], qt[0:nk].astype(BF16), preferred_element_type=F32)
    s2_ref[...] = jnp.dot(kb2_ref[...], qt[nk:2 * nk].astype(BF16), preferred_element_type=F32)

    def chunk(tb, carry):
        for sub in range(ROUTE_LANES // LANES):
            lsl = pl.ds(pl.multiple_of(tb * ROUTE_LANES + sub * LANES, LANES), LANES)
            s2c_ref[...] = s2_ref[:, lsl]
            for h in range(PEER_HEADS):
                s2h_ref[h, :, lsl] = s2c_ref[pl.ds(h, N_KEYS, stride=PEER_HEADS), :]
        tsl = pl.ds(pl.multiple_of(tb * ROUTE_LANES, ROUTE_LANES), ROUTE_LANES)
        a = _top17_desc(s1_ref[:, tsl].reshape(N_KEYS, PEER_HEADS, ROUTE_LANES))
        b = _top17_desc(s2_ref[:, tsl].reshape(N_KEYS, PEER_HEADS, ROUTE_LANES))
        nt = PEER_TOPK + 1
        cands = [a[p] + b[q] for p in range(nt) for q in range(nt) if (p + 1) * (q + 1) <= nt]
        order = cands + [None] * (64 - len(cands))
        _bitonic_sort_desc(order)
        thr = 0.5 * (order[PEER_TOPK - 1] + order[PEER_TOPK])
        m = order[0]
        z = functools.reduce(
            jnp.add, [jnp.where(cv >= thr, jnp.exp(cv - m), 0.0) for cv in cands])
        st_ref[0, :, tsl] = thr
        st_ref[1, :, tsl] = a[0]
        st_ref[2, :, tsl] = b[0]
        st_ref[3, :, tsl] = 1.0 / z
        return carry

    lax.fori_loop(0, tm // ROUTE_LANES, chunk, 0)


def _peer_route(h2t, wq_t, kb1, kb2, tm=512):
    d, n = h2t.shape
    nk = N_KEYS * PEER_HEADS
    s_shape = jax.ShapeDtypeStruct((nk, n), F32)
    return pl.pallas_call(
        functools.partial(_route_kernel, tm=tm),
        grid=(n // tm,),
        in_specs=[pl.BlockSpec((d, tm), lambda i: (0, i)),
                  pl.BlockSpec((2 * nk, d), lambda i: (0, 0)),
                  pl.BlockSpec((nk, nk), lambda i: (0, 0)),
                  pl.BlockSpec((nk, nk), lambda i: (0, 0))],
        out_specs=[pl.BlockSpec((nk, tm), lambda i: (0, i)),
                   pl.BlockSpec((PEER_HEADS, N_KEYS, tm), lambda i: (0, 0, i)),
                   pl.BlockSpec((4, PEER_HEADS, tm), lambda i: (0, 0, i))],
        out_shape=[s_shape, jax.ShapeDtypeStruct((PEER_HEADS, N_KEYS, n), F32),
                   jax.ShapeDtypeStruct((4, PEER_HEADS, n), F32)],
        scratch_shapes=[pltpu.VMEM((nk, tm), F32),
                        pltpu.VMEM((nk, LANES), F32)],
        compiler_params=_cparams(("parallel",)),
        name="peer_route",
    )(h2t, wq_t, kb1, kb2)


def _peer_kernel(ht_ref, u_ref, vt_ref, s1_ref, s2h_ref, st_ref, acc_ref,
                 e2_ref, c1_ref, th_ref, z_ref, g_ref, w_ref, *, tm, te, ne, n_steps):
    s = pl.program_id(0)
    e1 = jnp.minimum(s, n_steps - 1) % ne
    prev = jnp.maximum(s - 1, 0)
    ni = te // N_KEYS
    slot = s % 2

    @pl.when(s == 0)
    def _():
        w_ref[...] = jnp.zeros(w_ref.shape, w_ref.dtype)

    @pl.when(prev % ne == 0)
    def _():
        acc_ref[...] = jnp.zeros(acc_ref.shape, F32)

    @pl.when(e1 == 0)
    def _():
        for h in range(PEER_HEADS):
            e2_ref[h] = jnp.exp(s2h_ref[h] - st_ref[2, h:h + 1, :])
        c1_ref[...] = jnp.exp(s1_ref[...] - st_ref[1][None]) * st_ref[3][None]
        th_ref[...] = st_ref[0][None] - s1_ref[...]

    z_ref[...] = jnp.dot(u_ref[...], ht_ref[...], preferred_element_type=F32)
    acc_ref[...] += jnp.dot(vt_ref[...], w_ref[1 - slot], preferred_element_type=F32)

    for i in range(ni):
        key1 = e1 * ni + i
        for tb in range(tm // LANES):
            tsl = slice(tb * LANES, (tb + 1) * LANES)
            gate = None
            for h in range(PEER_HEADS):
                hit = s2h_ref[h, :, tsl] >= th_ref[key1, h:h + 1, tsl]
                term = jnp.where(hit, e2_ref[h, :, tsl] * c1_ref[key1, h:h + 1, tsl], 0.0)
                gate = term if gate is None else gate + term
            g_ref[i * N_KEYS:(i + 1) * N_KEYS, tsl] = gate

    w_ref[slot] = (_gelu_sigmoid_form(z_ref[...]) * g_ref[...]).astype(w_ref.dtype)


def _peer_dense(h2t, u, vt, s1, s2h, st, tm=512, te=1024):
    d, n = h2t.shape
    ne = u.shape[0] // te
    nk = N_KEYS * PEER_HEADS
    n_steps = (n // tm) * ne

    def cur(s):
        return jnp.minimum(s, n_steps - 1)

    def prev(s):
        return jnp.maximum(s - 1, 0)

    return pl.pallas_call(
        functools.partial(_peer_kernel, tm=tm, te=te, ne=ne, n_steps=n_steps),
        grid=(n_steps + 1,),
        in_specs=[pl.BlockSpec((d, tm), lambda s: (0, cur(s) // ne)),
                  pl.BlockSpec((te, d), lambda s: (cur(s) % ne, 0)),
                  pl.BlockSpec((d, te), lambda s: (0, prev(s) % ne)),
                  pl.BlockSpec((N_KEYS, PEER_HEADS, tm), lambda s: (0, 0, cur(s) // ne)),
                  pl.BlockSpec((PEER_HEADS, N_KEYS, tm), lambda s: (0, 0, cur(s) // ne)),
                  pl.BlockSpec((4, PEER_HEADS, tm), lambda s: (0, 0, cur(s) // ne))],
        out_specs=pl.BlockSpec((d, tm), lambda s: (0, prev(s) // ne)),
        out_shape=jax.ShapeDtypeStruct((d, n), F32),
        scratch_shapes=[pltpu.VMEM((PEER_HEADS, N_KEYS, tm), F32),
                        pltpu.VMEM((N_KEYS, PEER_HEADS, tm), F32),
                        pltpu.VMEM((N_KEYS, PEER_HEADS, tm), F32),
                        pltpu.VMEM((te, tm), F32),
                        pltpu.VMEM((te, tm), F32),
                        pltpu.VMEM((2, te, tm), BF16)],
        compiler_params=_cparams(("arbitrary",)),
        name="peer_dense",
    )(h2t, u, vt, s1, s2h, st)


def _final_kernel(x1_ref, pt_ref, o_ref):
    o_ref[...] = x1_ref[...] + pt_ref[...].T


def _final_add(x1, peer_t, tm=512):
    n, d = x1.shape
    return pl.pallas_call(
        _final_kernel,
        grid=(n // tm,),
        in_specs=[pl.BlockSpec((tm, d), lambda i: (i, 0)),
                  pl.BlockSpec((d, tm), lambda i: (0, i))],
        out_specs=pl.BlockSpec((tm, d), lambda i: (i, 0)),
        out_shape=jax.ShapeDtypeStruct((n, d), F32),
        compiler_params=_cparams(("parallel",)),
        name="final_residual",
    )(x1, peer_t)


def _block_diag(w):
    g, a, b = w.shape
    eye = jnp.eye(g, dtype=w.dtype)
    return jnp.einsum("gab,gh->gahb", w, eye).reshape(g * a, g * b)


def _key_matrix(keys_c):
    h, k, dh = keys_c.shape
    eye = jnp.eye(h, dtype=keys_c.dtype)
    return jnp.einsum("hkd,hg->khgd", keys_c, eye).reshape(k * h, h * dh)


def kernel(x, mem, norm1_g, w_in, attn_q_norm_g, attn_k_norm_g, rel_bias, conv_w, conv_b, lru_wa, lru_ba, lru_wx, lru_bx, lru_lambda, mem_norm_g, w_mem_kv, xq_norm_g, xk_norm_g, w_gate, b_gate, w_branch, w_o, norm2_g, w_peer_q, peer_keys, peer_u, peer_v):
    batch, seq, d = x.shape
    n = batch * seq
    aw = A_HEADS * HEAD_DIM
    cw = C_HEADS * HEAD_DIM
    lw = LRU_WIDTH
    x2 = x.reshape(n, d)

    cuts = np.cumsum([aw, aw, aw, lw, lw])
    wq, wk, wv, wxl, wgl, wqc = jnp.split(w_in, cuts, axis=1)
    w_a = jnp.concatenate([wq, wk, wv, wqc], axis=1).astype(BF16)
    w_b = jnp.concatenate([wxl, wgl], axis=1).astype(BF16)
    gains_a = jnp.concatenate([jnp.tile(attn_q_norm_g, A_HEADS), jnp.tile(attn_k_norm_g, A_HEADS),
                               jnp.ones((aw,), F32), jnp.tile(xq_norm_g, C_HEADS)]).reshape(1, -1)
    wbd = jnp.concatenate([_block_diag(lru_wa), _block_diag(lru_wx)], axis=1).astype(BF16)
    bab = jnp.concatenate([lru_ba, lru_bx]).reshape(1, 2 * lw)
    table = _attn_bias_table(rel_bias)
    hd = PEER_HEADS * N_KEYS
    wq_t = w_peer_q.reshape(d, PEER_HEADS, 2, HEAD_DIM).transpose(2, 1, 3, 0).reshape(2 * hd, d).astype(BF16)
    kb1 = _key_matrix(peer_keys[:, 0]).astype(BF16)
    kb2 = _key_matrix(peer_keys[:, 1]).astype(BF16)
    u_b = peer_u.astype(BF16)
    vt_b = peer_v.T.astype(BF16)

    h = _rmsnorm(x2, norm1_g)
    proj_a = _proj(h, w_a, gains_a, (2 * aw, 3 * aw), BF16, "in_proj_qkv")
    xg = _proj(h, w_b, jnp.ones((1, 2 * lw), F32), (0, 2 * lw), F32, "in_proj_lru")
    o_a = _band_attention(proj_a, table, batch, seq, qcol=0, kcol=1, vcol=2)
    o_b = _lru_branch(xg, conv_w, conv_b, wbd, bab, lru_lambda, batch, seq)
    k_mem, v_mem = _mem_kv(mem, mem_norm_g, w_mem_kv.astype(BF16), xk_norm_g)
    merged = _merge(h, o_a, o_b, proj_a, (3 * aw) // cw, k_mem, v_mem, w_gate.astype(BF16),
                    b_gate.reshape(3, d), w_branch.astype(BF16), seq)
    x1, h2t = _out_proj(x2, merged, w_o.astype(BF16), norm2_g)

    s1, s2h, st = _peer_route(h2t, wq_t, kb1, kb2)
    peer_t = _peer_dense(h2t, u_b, vt_b, s1.reshape(N_KEYS, PEER_HEADS, n), s2h, st)
    out = _final_add(x1, peer_t)
    return out.reshape(batch, seq, d)
```

```python
import functools

import numpy as np
import jax
import jax.numpy as jnp
from jax import lax
from jax.experimental import pallas as pl
from jax.experimental.pallas import tpu as pltpu

F32 = jnp.float32
BF16 = jnp.bfloat16

EPS = 1e-6
NEG = -1e30
LANES = 128
SUBLANES = 8
VMEM_LIMIT = 56 * 1024 * 1024
PEER_VMEM_LIMIT = 60 * 1024 * 1024

CHUNK = 64
N_PREV_CHUNKS = 8
REL_CLIP = 128
A_HEADS = 8
HEAD_DIM = 128
C_HEADS = 4
LRU_WIDTH = 512
LRU_C = 8.0
CONV_WIDTH = 4
PEER_HEADS = 8
N_KEYS = 128
PEER_TOPK = 16

ROUTE_LANES = 256
ATT_TQ = 256
ATT_TK = ATT_TQ + N_PREV_CHUNKS * CHUNK


def _cparams(sem):
    return pltpu.CompilerParams(dimension_semantics=sem, vmem_limit_bytes=VMEM_LIMIT)


def _gelu(x):
    return 0.5 * x * (1.0 + jnp.tanh(0.7978845608028654 * (x + 0.044715 * (x * x * x))))


_GELU_K1 = -2.0 * 0.7978845608028654 * 1.4426950408889634
_GELU_K2 = _GELU_K1 * 0.044715


def _gelu_sigmoid_form(x):
    return x / (1.0 + jnp.exp2(x * (_GELU_K1 + _GELU_K2 * (x * x))))


def _sigmoid(x):
    return 1.0 / (1.0 + jnp.exp(-x))


def _rmsnorm_kernel(x_ref, g_ref, o_ref):
    x = x_ref[...]
    ms = jnp.mean(x * x, axis=-1, keepdims=True)
    o_ref[...] = (x * lax.rsqrt(ms + EPS) * g_ref[...]).astype(o_ref.dtype)


def _rmsnorm(x, g, tr=512):
    n, d = x.shape
    return pl.pallas_call(
        _rmsnorm_kernel,
        grid=(n // tr,),
        in_specs=[pl.BlockSpec((tr, d), lambda i: (i, 0)),
                  pl.BlockSpec((1, d), lambda i: (0, 0))],
        out_specs=pl.BlockSpec((tr, d), lambda i: (i, 0)),
        out_shape=jax.ShapeDtypeStruct((n, d), BF16),
        compiler_params=_cparams(("parallel",)),
        name="rmsnorm1",
    )(x, g.reshape(1, d))


def _proj_kernel(h_ref, w_ref, g_ref, o_ref, *, plain_lo, plain_hi, tn):
    acc = jnp.dot(h_ref[...], w_ref[...], preferred_element_type=F32)
    j = pl.program_id(1)
    plain = (j >= plain_lo) & (j < plain_hi)

    @pl.when(jnp.logical_not(plain))
    def _():
        for c in range(tn // HEAD_DIM):
            sl = slice(c * HEAD_DIM, (c + 1) * HEAD_DIM)
            blk = acc[:, sl]
            ms = jnp.mean(blk * blk, axis=-1, keepdims=True)
            o_ref[:, sl] = (blk * lax.rsqrt(ms + EPS) * g_ref[:, sl]).astype(o_ref.dtype)

    @pl.when(plain)
    def _():
        o_ref[...] = acc.astype(o_ref.dtype)


def _proj(h, w, w_col_tile, gains, plain_cols, out_dtype, name, tm=1024, tn=512):
    n, d = h.shape
    nc = gains.shape[1]
    kern = functools.partial(_proj_kernel, plain_lo=plain_cols[0] // tn, plain_hi=plain_cols[1] // tn, tn=tn)
    return pl.pallas_call(
        kern,
        grid=(n // tm, nc // tn),
        in_specs=[pl.BlockSpec((tm, d), lambda i, j: (i, 0)),
                  pl.BlockSpec((d, tn), lambda i, j: (0, w_col_tile(j))),
                  pl.BlockSpec((1, tn), lambda i, j: (0, j))],
        out_specs=pl.BlockSpec((tm, tn), lambda i, j: (i, j)),
        out_shape=jax.ShapeDtypeStruct((n, nc), out_dtype),
        compiler_params=_cparams(("parallel", "arbitrary")),
        name=name,
    )(h, w, gains)


def _attn_kernel(q_ref, k0_ref, k1_ref, k2_ref, v0_ref, v1_ref, v2_ref, t_ref, o_ref):
    nt = (((1,), (1,)), ((), ()))
    for h in range(A_HEADS):
        sl = slice(h * HEAD_DIM, (h + 1) * HEAD_DIM)
        q = q_ref[:, sl]
        s = jnp.concatenate(
            [lax.dot_general(q, kr[:, sl], nt, preferred_element_type=F32)
             for kr in (k0_ref, k1_ref, k2_ref)], axis=1)
        s = s + t_ref[0, h]
        m = jnp.max(s, axis=-1, keepdims=True)
        p = jnp.exp(s - m)
        l = jnp.sum(p, axis=-1, keepdims=True)
        pb = p.astype(BF16)
        o = jnp.dot(pb[:, 0:ATT_TQ], v0_ref[:, sl], preferred_element_type=F32)
        o += jnp.dot(pb[:, ATT_TQ:2 * ATT_TQ], v1_ref[:, sl], preferred_element_type=F32)
        o += jnp.dot(pb[:, 2 * ATT_TQ:], v2_ref[:, sl], preferred_element_type=F32)
        o_ref[:, sl] = (o / l).astype(o_ref.dtype)


def _attn_bias_table(rel_bias):
    nh = rel_bias.shape[0]
    tq, tk = ATT_TQ, 3 * ATT_TQ
    qi = np.arange(tq)[:, None]
    kj = np.arange(tk)[None, :]
    qch = qi // CHUNK
    kch = kj // CHUNK - (2 * tq) // CHUNK
    band = (kch <= qch) & (kch >= qch - N_PREV_CHUNKS)
    n_far = 3 * tq - 1 - REL_CLIP + 1
    n_near = tq + tk - 1 - n_far - (2 * REL_CLIP - 1)
    g = jnp.concatenate([jnp.broadcast_to(rel_bias[:, 2 * REL_CLIP:], (nh, n_far)),
                         rel_bias[:, 1:2 * REL_CLIP][:, ::-1],
                         jnp.broadcast_to(rel_bias[:, :1], (nh, n_near))], axis=1).astype(F32)
    ln = tq + tk
    gp = jnp.pad(g, ((0, 0), (0, 1)))
    m = jnp.tile(gp, (1, tq))[:, :tq * (ln - 1)].reshape(nh, tq, ln - 1)
    bias = m[:, :, tq - 1:tq - 1 + tk]
    tabs = []
    for t in range(3):
        valid = band & (kj >= 2 * ATT_TQ - ATT_TQ * t)
        tabs.append(jnp.where(valid[None], bias, NEG))
    return jnp.stack(tabs)


def _band_attention(proj, table, batch, seq, qcol, kcol, vcol):
    n = proj.shape[0]
    aw = A_HEADS * HEAD_DIM
    nq = seq // ATT_TQ

    def qmap(b, t):
        return (b * nq + t, qcol)

    def kvmap(col, back):
        return lambda b, t: (b * nq + jnp.maximum(t - back, 0), col)

    blk = (ATT_TQ, aw)
    return pl.pallas_call(
        _attn_kernel,
        grid=(batch, nq),
        in_specs=[pl.BlockSpec(blk, qmap),
                  pl.BlockSpec(blk, kvmap(kcol, 2)), pl.BlockSpec(blk, kvmap(kcol, 1)),
                  pl.BlockSpec(blk, kvmap(kcol, 0)),
                  pl.BlockSpec(blk, kvmap(vcol, 2)), pl.BlockSpec(blk, kvmap(vcol, 1)),
                  pl.BlockSpec(blk, kvmap(vcol, 0)),
                  pl.BlockSpec((1, A_HEADS, ATT_TQ, 3 * ATT_TQ),
                               lambda b, t: (jnp.minimum(t, 2), 0, 0, 0))],
        out_specs=pl.BlockSpec(blk, lambda b, t: (b * nq + t, 0)),
        out_shape=jax.ShapeDtypeStruct((n, aw), BF16),
        compiler_params=_cparams(("parallel", "arbitrary")),
        name="band_attention",
    )(proj, proj, proj, proj, proj, proj, proj, table)


def _lru_kernel(xg_ref, cw_ref, cb_ref, wbd_ref, bab_ref, lam_ref, o_ref,
                xpad_ref, hprev_ref, hbuf_ref, *, tt):
    w = LRU_WIDTH
    t = pl.program_id(1)

    @pl.when(t == 0)
    def _():
        xpad_ref[0:SUBLANES, :] = jnp.zeros((SUBLANES, w), F32)
        hprev_ref[...] = jnp.zeros((1, w), F32)

    xl = xg_ref[:, 0:w]
    gl = xg_ref[:, w:2 * w]
    xpad_ref[SUBLANES:SUBLANES + tt, :] = xl
    xc = cw_ref[3:4, :] * xl + cb_ref[...]
    for k in range(CONV_WIDTH - 1):
        off = SUBLANES - (CONV_WIDTH - 1) + k
        xc = xc + cw_ref[k:k + 1, :] * xpad_ref[off:off + tt, :]
    xpad_ref[0:SUBLANES, :] = xl[tt - SUBLANES:tt, :]

    z = jnp.dot(xc.astype(BF16), wbd_ref[...], preferred_element_type=F32) + bab_ref[...]
    r = _sigmoid(z[:, 0:w])
    i = _sigmoid(z[:, w:2 * w])
    nl = -lam_ref[...]
    softplus = jnp.maximum(nl, 0.0) + jnp.log(1.0 + jnp.exp(-jnp.abs(nl)))
    log_a = -LRU_C * r * softplus
    a = jnp.exp(log_a)
    y = jnp.clip(1.0 - a * a, 1e-12, 1.0)
    u = (y * lax.rsqrt(y)) * (i * xc)

    sub = lax.broadcasted_iota(jnp.int32, (tt, w), 0) % SUBLANES
    d = 1
    while d < SUBLANES:
        keep = sub >= d
        a_sh = jnp.where(keep, pltpu.roll(a, d, axis=0), 1.0)
        u_sh = jnp.where(keep, pltpu.roll(u, d, axis=0), 0.0)
        u = a * u_sh + u
        a = a * a_sh
        d *= 2
    carry = hprev_ref[...]
    for g in range(tt // SUBLANES):
        rows = slice(g * SUBLANES, (g + 1) * SUBLANES)
        h = u[rows] + a[rows] * carry
        carry = h[SUBLANES - 1:SUBLANES, :]
        hbuf_ref[rows, :] = h
    hprev_ref[...] = carry
    o_ref[...] = (hbuf_ref[...] * _gelu(gl)).astype(o_ref.dtype)


def _lru_branch(xg, conv_w, conv_b, wbd, bab, lam, batch, seq, tt=512):
    n = xg.shape[0]
    w = LRU_WIDTH
    nt = seq // tt
    const = lambda b, t: (0, 0)
    return pl.pallas_call(
        functools.partial(_lru_kernel, tt=tt),
        grid=(batch, nt),
        in_specs=[pl.BlockSpec((tt, 2 * w), lambda b, t: (b * nt + t, 0)),
                  pl.BlockSpec((CONV_WIDTH, w), const),
                  pl.BlockSpec((1, w), const),
                  pl.BlockSpec((w, 2 * w), const),
                  pl.BlockSpec((1, 2 * w), const),
                  pl.BlockSpec((1, w), const)],
        out_specs=pl.BlockSpec((tt, w), lambda b, t: (b * nt + t, 0)),
        out_shape=jax.ShapeDtypeStruct((n, w), BF16),
        scratch_shapes=[pltpu.VMEM((tt + SUBLANES, w), F32), pltpu.VMEM((1, w), F32),
                        pltpu.VMEM((tt, w), F32)],
        compiler_params=_cparams(("parallel", "arbitrary")),
        name="rg_lru",
    )(xg, conv_w, conv_b.reshape(1, w), wbd, bab, lam.reshape(1, w))


def _memkv_kernel(mem_ref, g_ref, w_ref, kg_ref, k_ref, v_ref):
    x = mem_ref[0]
    ms = jnp.mean(x * x, axis=-1, keepdims=True)
    mn = (x * lax.rsqrt(ms + EPS) * g_ref[...]).astype(BF16)
    kv = jnp.dot(mn, w_ref[...], preferred_element_type=F32)
    cw = C_HEADS * HEAD_DIM
    for h in range(C_HEADS):
        sl = slice(h * HEAD_DIM, (h + 1) * HEAD_DIM)
        blk = kv[:, sl]
        ms = jnp.mean(blk * blk, axis=-1, keepdims=True)
        k_ref[0, :, sl] = (blk * lax.rsqrt(ms + EPS) * kg_ref[...]).astype(k_ref.dtype)
    v_ref[0] = kv[:, cw:2 * cw].astype(v_ref.dtype)


def _mem_kv(mem, g, w_kv, kg):
    b, m, d = mem.shape
    cw = C_HEADS * HEAD_DIM
    out = jax.ShapeDtypeStruct((b, m, cw), BF16)
    return pl.pallas_call(
        _memkv_kernel,
        grid=(b,),
        in_specs=[pl.BlockSpec((1, m, d), lambda i: (i, 0, 0)),
                  pl.BlockSpec((1, d), lambda i: (0, 0)),
                  pl.BlockSpec((d, 2 * cw), lambda i: (0, 0)),
                  pl.BlockSpec((1, HEAD_DIM), lambda i: (0, 0))],
        out_specs=[pl.BlockSpec((1, m, cw), lambda i: (i, 0, 0)),
                   pl.BlockSpec((1, m, cw), lambda i: (i, 0, 0))],
        out_shape=[out, out],
        compiler_params=_cparams(("parallel",)),
        name="mem_kv",
    )(mem, g.reshape(1, d), w_kv, kg.reshape(1, HEAD_DIM))


def _merge_kernel(h_ref, oa_ref, ob_ref, qc_ref, km_ref, vm_ref, wg0_ref, wg1_ref, wg2_ref, bg_ref,
                  wba_ref, wbb_ref, wbc_ref, o_ref, oc_ref):
    j = pl.program_id(1)

    @pl.when(j == 0)
    def _():
        scale = HEAD_DIM ** -0.5
        nt = (((1,), (1,)), ((), ()))
        for hh in range(C_HEADS):
            sl = slice(hh * HEAD_DIM, (hh + 1) * HEAD_DIM)
            s = lax.dot_general(qc_ref[:, sl], km_ref[0, :, sl], nt, preferred_element_type=F32) * scale
            m = jnp.max(s, axis=-1, keepdims=True)
            p = jnp.exp(s - m)
            l = jnp.sum(p, axis=-1, keepdims=True)
            o = jnp.dot(p.astype(BF16), vm_ref[0, :, sl], preferred_element_type=F32)
            oc_ref[:, sl] = (o / l).astype(oc_ref.dtype)

    h = h_ref[...]
    merged = None
    for br, (o_br, wg_ref, wb_ref) in enumerate(((oa_ref[...], wg0_ref, wba_ref),
                                                 (ob_ref[...], wg1_ref, wbb_ref),
                                                 (oc_ref[...], wg2_ref, wbc_ref))):
        y = jnp.dot(o_br, wb_ref[...], preferred_element_type=F32)
        g = _sigmoid(jnp.dot(h, wg_ref[...], preferred_element_type=F32) + bg_ref[br:br + 1, :])
        merged = g * y if merged is None else merged + g * y
    o_ref[...] = merged.astype(o_ref.dtype)


def _merge(h, o_a, o_b, proj_a, qc_col, k_mem, v_mem, w_gate, b_gate3, w_branch, seq, tm=1024, tn=512):
    n, d = h.shape
    aw = A_HEADS * HEAD_DIM
    cw = C_HEADS * HEAD_DIM
    n_mem = k_mem.shape[1]
    ncol = d // tn
    tiles_per_seq = seq // tm
    return pl.pallas_call(
        _merge_kernel,
        grid=(n // tm, ncol),
        in_specs=[pl.BlockSpec((tm, d), lambda i, j: (i, 0)),
                  pl.BlockSpec((tm, aw), lambda i, j: (i, 0)),
                  pl.BlockSpec((tm, LRU_WIDTH), lambda i, j: (i, 0)),
                  pl.BlockSpec((tm, cw), lambda i, j: (i, qc_col)),
                  pl.BlockSpec((1, n_mem, cw), lambda i, j: (i // tiles_per_seq, 0, 0)),
                  pl.BlockSpec((1, n_mem, cw), lambda i, j: (i // tiles_per_seq, 0, 0)),
                  pl.BlockSpec((d, tn), lambda i, j: (0, j)),
                  pl.BlockSpec((d, tn), lambda i, j: (0, j + ncol)),
                  pl.BlockSpec((d, tn), lambda i, j: (0, j + 2 * ncol)),
                  pl.BlockSpec((3, tn), lambda i, j: (0, j)),
                  pl.BlockSpec((aw, tn), lambda i, j: (0, j)),
                  pl.BlockSpec((LRU_WIDTH, tn), lambda i, j: (aw // LRU_WIDTH, j)),
                  pl.BlockSpec((cw, tn), lambda i, j: ((aw + LRU_WIDTH) // cw, j))],
        out_specs=pl.BlockSpec((tm, tn), lambda i, j: (i, j)),
        out_shape=jax.ShapeDtypeStruct((n, d), BF16),
        scratch_shapes=[pltpu.VMEM((tm, cw), BF16)],
        compiler_params=_cparams(("parallel", "arbitrary")),
        name="gated_merge",
    )(h, o_a, o_b, proj_a, k_mem, v_mem, w_gate, w_gate, w_gate, b_gate3, w_branch, w_branch, w_branch)


def _oproj_kernel(x_ref, m_ref, w_ref, g_ref, x1_ref, h2t_ref):
    x1 = x_ref[...] + jnp.dot(m_ref[...], w_ref[...], preferred_element_type=F32)
    x1_ref[...] = x1
    ms = jnp.mean(x1 * x1, axis=-1, keepdims=True)
    h2 = x1 * lax.rsqrt(ms + EPS) * g_ref[...]
    h2t_ref[...] = h2.T.astype(h2t_ref.dtype)


def _out_proj(x, merged, w_o, g2, tm=512):
    n, d = x.shape
    return pl.pallas_call(
        _oproj_kernel,
        grid=(n // tm,),
        in_specs=[pl.BlockSpec((tm, d), lambda i: (i, 0)),
                  pl.BlockSpec((tm, d), lambda i: (i, 0)),
                  pl.BlockSpec((d, d), lambda i: (0, 0)),
                  pl.BlockSpec((1, d), lambda i: (0, 0))],
        out_specs=[pl.BlockSpec((tm, d), lambda i: (i, 0)),
                   pl.BlockSpec((d, tm), lambda i: (0, i))],
        out_shape=[jax.ShapeDtypeStruct((n, d), F32), jax.ShapeDtypeStruct((d, n), BF16)],
        compiler_params=_cparams(("parallel",)),
        name="out_proj_norm2",
    )(x, merged, w_o, g2.reshape(1, d))


def _cmpx(v, i, j):
    a, b = v[i], v[j]
    if b is None:
        return
    if a is None:
        v[i], v[j] = b, None
        return
    v[i], v[j] = jnp.maximum(a, b), jnp.minimum(a, b)


def _bitonic_merge_desc(v, lo, n):
    j = n // 2
    while j >= 1:
        for i in range(lo, lo + n):
            if (i - lo) & j == 0:
                _cmpx(v, i, i + j)
        j //= 2


def _bitonic_sort_desc(v):
    n = len(v)
    k = 2
    while k <= n:
        j = k // 2
        while j >= 1:
            for i in range(n):
                l = i ^ j
                if l > i:
                    if i & k == 0 or k == n:
                        _cmpx(v, i, l)
                    else:
                        _cmpx(v, l, i)
            j //= 2
        k *= 2


def _top17_desc(x):
    k = PEER_TOPK
    groups = []
    for g in range(x.shape[0] // k):
        run = [x[g * k + r] for r in range(k)]
        _bitonic_sort_desc(run)
        groups.append(run)
    dropped = None
    while len(groups) > 1:
        merged = []
        for g in range(0, len(groups), 2):
            p, q = groups[g], groups[g + 1]
            hi = [jnp.maximum(p[r], q[k - 1 - r]) for r in range(k)]
            lo = functools.reduce(jnp.maximum, [jnp.minimum(p[r], q[k - 1 - r]) for r in range(k)])
            dropped = lo if dropped is None else jnp.maximum(dropped, lo)
            _bitonic_merge_desc(hi, 0, k)
            merged.append(hi)
        groups = merged
    return groups[0] + [dropped]


def _route_kernel(ht_ref, wq_ref, kb1_ref, kb2_ref, s1_ref, s2h_ref, st_ref,
                  s2_ref, s2c_ref, *, tm):
    nk = N_KEYS * PEER_HEADS
    qt = jnp.dot(wq_ref[...], ht_ref[...], preferred_element_type=F32)
    s1_ref[...] = jnp.dot(kb1_ref[...], qt[0:nk].astype(BF16), preferred_element_type=F32)
    s2_ref[...] = jnp.dot(kb2_ref[...], qt[nk:2 * nk].astype(BF16), preferred_element_type=F32)

    def chunk(tb, carry):
        for sub in range(ROUTE_LANES // LANES):
            lsl = pl.ds(pl.multiple_of(tb * ROUTE_LANES + sub * LANES, LANES), LANES)
            s2c_ref[...] = s2_ref[:, lsl]
            for h in range(PEER_HEADS):
                s2h_ref[h, :, lsl] = s2c_ref[pl.ds(h, N_KEYS, stride=PEER_HEADS), :]
        tsl = pl.ds(pl.multiple_of(tb * ROUTE_LANES, ROUTE_LANES), ROUTE_LANES)
        a = _top17_desc(s1_ref[:, tsl].reshape(N_KEYS, PEER_HEADS, ROUTE_LANES))
        b = _top17_desc(s2_ref[:, tsl].reshape(N_KEYS, PEER_HEADS, ROUTE_LANES))
        nt = PEER_TOPK + 1
        cands = [a[p] + b[q] for p in range(nt) for q in range(nt) if (p + 1) * (q + 1) <= nt]
        order = cands + [None] * (64 - len(cands))
        _bitonic_sort_desc(order)
        thr = 0.5 * (order[PEER_TOPK - 1] + order[PEER_TOPK])
        m = order[0]
        z = functools.reduce(
            jnp.add, [jnp.where(cv >= thr, jnp.exp(cv - m), 0.0) for cv in cands])
        st_ref[0, :, tsl] = thr
        st_ref[1, :, tsl] = a[0]
        st_ref[2, :, tsl] = b[0]
        st_ref[3, :, tsl] = 1.0 / z
        return carry

    lax.fori_loop(0, tm // ROUTE_LANES, chunk, 0)


def _peer_route(h2t, wq_t, kb1, kb2, tm=512):
    d, n = h2t.shape
    nk = N_KEYS * PEER_HEADS
    s_shape = jax.ShapeDtypeStruct((nk, n), F32)
    return pl.pallas_call(
        functools.partial(_route_kernel, tm=tm),
        grid=(n // tm,),
        in_specs=[pl.BlockSpec((d, tm), lambda i: (0, i)),
                  pl.BlockSpec((2 * nk, d), lambda i: (0, 0)),
                  pl.BlockSpec((nk, nk), lambda i: (0, 0)),
                  pl.BlockSpec((nk, nk), lambda i: (0, 0))],
        out_specs=[pl.BlockSpec((nk, tm), lambda i: (0, i)),
                   pl.BlockSpec((PEER_HEADS, N_KEYS, tm), lambda i: (0, 0, i)),
                   pl.BlockSpec((4, PEER_HEADS, tm), lambda i: (0, 0, i))],
        out_shape=[s_shape, jax.ShapeDtypeStruct((PEER_HEADS, N_KEYS, n), F32),
                   jax.ShapeDtypeStruct((4, PEER_HEADS, n), F32)],
        scratch_shapes=[pltpu.VMEM((nk, tm), F32),
                        pltpu.VMEM((nk, LANES), F32)],
        compiler_params=_cparams(("parallel",)),
        name="peer_route",
    )(h2t, wq_t, kb1, kb2)


def _peer_kernel(ht_ref, u_ref, v_ref, s1_ref, s2h_ref, st_ref, x1_ref, acc_ref,
                 e2_ref, c1_ref, th_ref, z_ref, g_ref, w_ref, *, tm, te, ne, n_steps):
    s = pl.program_id(0)
    e1 = jnp.minimum(s, n_steps - 1) % ne
    prev = jnp.maximum(s - 1, 0)
    ni = te // N_KEYS
    slot = s % 2

    @pl.when(s == 0)
    def _():
        w_ref[...] = jnp.zeros(w_ref.shape, w_ref.dtype)

    @pl.when(prev % ne == 0)
    def _():
        acc_ref[...] = x1_ref[...]

    @pl.when(e1 == 0)
    def _():
        for h in range(PEER_HEADS):
            e2_ref[h] = jnp.exp(s2h_ref[h] - st_ref[2, h:h + 1, :])
        c1_ref[...] = jnp.exp(s1_ref[...] - st_ref[1][None]) * st_ref[3][None]
        th_ref[...] = st_ref[0][None] - s1_ref[...]

    for i in range(ni):
        key1 = e1 * ni + i
        for tb in range(tm // LANES):
            tsl = slice(tb * LANES, (tb + 1) * LANES)
            gate = None
            for h in range(PEER_HEADS):
                hit = s2h_ref[h, :, tsl] >= th_ref[key1, h:h + 1, tsl]
                term = jnp.where(hit, e2_ref[h, :, tsl] * c1_ref[key1, h:h + 1, tsl], 0.0)
                gate = term if gate is None else gate + term
            g_ref[i * N_KEYS:(i + 1) * N_KEYS, tsl] = gate

    z_ref[...] = jnp.dot(u_ref[...], ht_ref[...], preferred_element_type=F32)
    acc_ref[...] += lax.dot_general(w_ref[1 - slot], v_ref[...], (((0,), (0,)), ((), ())),
                                    preferred_element_type=F32)

    w_ref[slot] = (_gelu_sigmoid_form(z_ref[...]) * g_ref[...]).astype(w_ref.dtype)


def _peer_dense(h2t, u, v, s1, s2h, st, x1, tm=512, te=1024):
    d, n = h2t.shape
    ne = u.shape[0] // te
    n_steps = (n // tm) * ne

    def cur(s):
        return jnp.minimum(s, n_steps - 1)

    def prev(s):
        return jnp.maximum(s - 1, 0)

    return pl.pallas_call(
        functools.partial(_peer_kernel, tm=tm, te=te, ne=ne, n_steps=n_steps),
        grid=(n_steps + 1,),
        in_specs=[pl.BlockSpec((d, tm), lambda s: (0, cur(s) // ne)),
                  pl.BlockSpec((te, d), lambda s: (cur(s) % ne, 0)),
                  pl.BlockSpec((te, d), lambda s: (prev(s) % ne, 0)),
                  pl.BlockSpec((N_KEYS, PEER_HEADS, tm), lambda s: (0, 0, cur(s) // ne)),
                  pl.BlockSpec((PEER_HEADS, N_KEYS, tm), lambda s: (0, 0, cur(s) // ne)),
                  pl.BlockSpec((4, PEER_HEADS, tm), lambda s: (0, 0, cur(s) // ne)),
                  pl.BlockSpec((tm, d), lambda s: (prev(s) // ne, 0))],
        out_specs=pl.BlockSpec((tm, d), lambda s: (prev(s) // ne, 0)),
        out_shape=jax.ShapeDtypeStruct((n, d), F32),
        scratch_shapes=[pltpu.VMEM((PEER_HEADS, N_KEYS, tm), F32),
                        pltpu.VMEM((N_KEYS, PEER_HEADS, tm), F32),
                        pltpu.VMEM((N_KEYS, PEER_HEADS, tm), F32),
                        pltpu.VMEM((te, tm), F32),
                        pltpu.VMEM((te, tm), F32),
                        pltpu.VMEM((2, te, tm), BF16)],
        compiler_params=pltpu.CompilerParams(dimension_semantics=("arbitrary",),
                                             vmem_limit_bytes=PEER_VMEM_LIMIT),
        name="peer_dense",
    )(h2t, u, v, s1, s2h, st, x1)


def _block_diag(w):
    g, a, b = w.shape
    eye = jnp.eye(g, dtype=w.dtype)
    return jnp.einsum("gab,gh->gahb", w, eye).reshape(g * a, g * b)


def _key_matrix(keys_c):
    h, k, dh = keys_c.shape
    eye = jnp.eye(h, dtype=keys_c.dtype)
    return jnp.einsum("hkd,hg->khgd", keys_c, eye).reshape(k * h, h * dh)


def kernel(x, mem, norm1_g, w_in, attn_q_norm_g, attn_k_norm_g, rel_bias, conv_w, conv_b, lru_wa, lru_ba, lru_wx, lru_bx, lru_lambda, mem_norm_g, w_mem_kv, xq_norm_g, xk_norm_g, w_gate, b_gate, w_branch, w_o, norm2_g, w_peer_q, peer_keys, peer_u, peer_v):
    batch, seq, d = x.shape
    n = batch * seq
    aw = A_HEADS * HEAD_DIM
    cw = C_HEADS * HEAD_DIM
    lw = LRU_WIDTH
    x2 = x.reshape(n, d)

    w_in_b = w_in.astype(BF16)
    tn = 512
    lru_tile0, qc_tile0 = 3 * aw // tn, (3 * aw + 2 * lw) // tn
    gains_a = jnp.concatenate([jnp.tile(attn_q_norm_g * HEAD_DIM ** -0.5, A_HEADS),
                               jnp.tile(attn_k_norm_g, A_HEADS),
                               jnp.ones((aw,), F32), jnp.tile(xq_norm_g, C_HEADS)]).reshape(1, -1)
    wbd = jnp.concatenate([_block_diag(lru_wa), _block_diag(lru_wx)], axis=1).astype(BF16)
    bab = jnp.concatenate([lru_ba, lru_bx]).reshape(1, 2 * lw)
    table = _attn_bias_table(rel_bias)
    hd = PEER_HEADS * N_KEYS
    wq_t = w_peer_q.reshape(d, PEER_HEADS, 2, HEAD_DIM).transpose(2, 1, 3, 0).reshape(2 * hd, d).astype(BF16)
    kb1 = _key_matrix(peer_keys[:, 0]).astype(BF16)
    kb2 = _key_matrix(peer_keys[:, 1]).astype(BF16)
    u_b = peer_u.astype(BF16)
    v_b = peer_v.astype(BF16)

    h = _rmsnorm(x2, norm1_g)
    proj_a = _proj(h, w_in_b, lambda j: jnp.where(j < lru_tile0, j, j + (qc_tile0 - lru_tile0)),
                   gains_a, (2 * aw, 3 * aw), BF16, "in_proj_qkv", tn=tn)
    xg = _proj(h, w_in_b, lambda j: j + lru_tile0, jnp.ones((1, 2 * lw), F32),
               (0, 2 * lw), F32, "in_proj_lru", tn=tn)
    o_a = _band_attention(proj_a, table, batch, seq, qcol=0, kcol=1, vcol=2)
    o_b = _lru_branch(xg, conv_w, conv_b, wbd, bab, lru_lambda, batch, seq)
    k_mem, v_mem = _mem_kv(mem, mem_norm_g, w_mem_kv.astype(BF16), xk_norm_g)
    merged = _merge(h, o_a, o_b, proj_a, (3 * aw) // cw, k_mem, v_mem, w_gate.astype(BF16),
                    b_gate.reshape(3, d), w_branch.astype(BF16), seq)
    x1, h2t = _out_proj(x2, merged, w_o.astype(BF16), norm2_g)

    s1, s2h, st = _peer_route(h2t, wq_t, kb1, kb2)
    out = _peer_dense(h2t, u_b, v_b, s1.reshape(N_KEYS, PEER_HEADS, n), s2h, st, x1)
    return out.reshape(batch, seq, d)
```

```python
import functools

import numpy as np
import jax
import jax.numpy as jnp
from jax import lax
from jax.experimental import pallas as pl
from jax.experimental.pallas import tpu as pltpu

F32 = jnp.float32
BF16 = jnp.bfloat16

EPS = 1e-6
NEG = -1e30
LANES = 128
SUBLANES = 8
VMEM_LIMIT = 56 * 1024 * 1024

CHUNK = 64
N_PREV_CHUNKS = 8
REL_CLIP = 128
A_HEADS = 8
HEAD_DIM = 128
C_HEADS = 4
LRU_WIDTH = 512
LRU_C = 8.0
CONV_WIDTH = 4
PEER_HEADS = 8
N_KEYS = 128
PEER_TOPK = 16

PEER_GATE_ROWS = 64
ROUTE_LANES = 256
ATT_TQ = 256
ATT_TK = ATT_TQ + N_PREV_CHUNKS * CHUNK


def _cparams(sem):
    return pltpu.CompilerParams(dimension_semantics=sem, vmem_limit_bytes=VMEM_LIMIT)


def _gelu(x):
    return 0.5 * x * (1.0 + jnp.tanh(0.7978845608028654 * (x + 0.044715 * (x * x * x))))


_GELU_K1 = -2.0 * 0.7978845608028654 * 1.4426950408889634
_GELU_K2 = _GELU_K1 * 0.044715


def _gelu_sigmoid_form(x):
    return x / (1.0 + jnp.exp2(x * (_GELU_K1 + _GELU_K2 * (x * x))))


def _sigmoid(x):
    return 1.0 / (1.0 + jnp.exp(-x))


def _rmsnorm_kernel(x_ref, g_ref, o_ref):
    x = x_ref[...]
    ms = jnp.mean(x * x, axis=-1, keepdims=True)
    o_ref[...] = (x * lax.rsqrt(ms + EPS) * g_ref[...]).astype(o_ref.dtype)


def _rmsnorm(x, g, tr=512):
    n, d = x.shape
    return pl.pallas_call(
        _rmsnorm_kernel,
        grid=(n // tr,),
        in_specs=[pl.BlockSpec((tr, d), lambda i: (i, 0)),
                  pl.BlockSpec((1, d), lambda i: (0, 0))],
        out_specs=pl.BlockSpec((tr, d), lambda i: (i, 0)),
        out_shape=jax.ShapeDtypeStruct((n, d), BF16),
        compiler_params=_cparams(("parallel",)),
        name="rmsnorm1",
    )(x, g.reshape(1, d))


def _proj_kernel(h_ref, w_ref, g_ref, o_ref, *, plain_lo, plain_hi, tn):
    acc = jnp.dot(h_ref[...], w_ref[...], preferred_element_type=F32)
    j = pl.program_id(1)
    plain = (j >= plain_lo) & (j < plain_hi)

    @pl.when(jnp.logical_not(plain))
    def _():
        for c in range(tn // HEAD_DIM):
            sl = slice(c * HEAD_DIM, (c + 1) * HEAD_DIM)
            blk = acc[:, sl]
            ms = jnp.mean(blk * blk, axis=-1, keepdims=True)
            o_ref[:, sl] = (blk * lax.rsqrt(ms + EPS) * g_ref[:, sl]).astype(o_ref.dtype)

    @pl.when(plain)
    def _():
        o_ref[...] = acc.astype(o_ref.dtype)


def _proj(h, w, w_col_tile, gains, plain_cols, out_dtype, name, tm=1024, tn=512):
    n, d = h.shape
    nc = gains.shape[1]
    kern = functools.partial(_proj_kernel, plain_lo=plain_cols[0] // tn, plain_hi=plain_cols[1] // tn, tn=tn)
    return pl.pallas_call(
        kern,
        grid=(n // tm, nc // tn),
        in_specs=[pl.BlockSpec((tm, d), lambda i, j: (i, 0)),
                  pl.BlockSpec((d, tn), lambda i, j: (0, w_col_tile(j))),
                  pl.BlockSpec((1, tn), lambda i, j: (0, j))],
        out_specs=pl.BlockSpec((tm, tn), lambda i, j: (i, j)),
        out_shape=jax.ShapeDtypeStruct((n, nc), out_dtype),
        compiler_params=_cparams(("parallel", "arbitrary")),
        name=name,
    )(h, w, gains)


def _attn_kernel(q_ref, k0_ref, k1_ref, k2_ref, v0_ref, v1_ref, v2_ref, t_ref, o_ref):
    nt = (((1,), (1,)), ((), ()))
    for h in range(A_HEADS):
        sl = slice(h * HEAD_DIM, (h + 1) * HEAD_DIM)
        q = q_ref[:, sl]
        s = jnp.concatenate(
            [lax.dot_general(q, kr[:, sl], nt, preferred_element_type=F32)
             for kr in (k0_ref, k1_ref, k2_ref)], axis=1)
        s = s + t_ref[0, h]
        m = jnp.max(s, axis=-1, keepdims=True)
        p = jnp.exp(s - m)
        l = jnp.sum(p, axis=-1, keepdims=True)
        pb = p.astype(BF16)
        o = jnp.dot(pb[:, 0:ATT_TQ], v0_ref[:, sl], preferred_element_type=F32)
        o += jnp.dot(pb[:, ATT_TQ:2 * ATT_TQ], v1_ref[:, sl], preferred_element_type=F32)
        o += jnp.dot(pb[:, 2 * ATT_TQ:], v2_ref[:, sl], preferred_element_type=F32)
        o_ref[:, sl] = (o / l).astype(o_ref.dtype)


def _attn_bias_table(rel_bias):
    nh = rel_bias.shape[0]
    tq, tk = ATT_TQ, 3 * ATT_TQ
    qi = np.arange(tq)[:, None]
    kj = np.arange(tk)[None, :]
    qch = qi // CHUNK
    kch = kj // CHUNK - (2 * tq) // CHUNK
    band = (kch <= qch) & (kch >= qch - N_PREV_CHUNKS)
    n_far = 3 * tq - 1 - REL_CLIP + 1
    n_near = tq + tk - 1 - n_far - (2 * REL_CLIP - 1)
    g = jnp.concatenate([jnp.broadcast_to(rel_bias[:, 2 * REL_CLIP:], (nh, n_far)),
                         rel_bias[:, 1:2 * REL_CLIP][:, ::-1],
                         jnp.broadcast_to(rel_bias[:, :1], (nh, n_near))], axis=1).astype(F32)
    ln = tq + tk
    gp = jnp.pad(g, ((0, 0), (0, 1)))
    m = jnp.tile(gp, (1, tq))[:, :tq * (ln - 1)].reshape(nh, tq, ln - 1)
    bias = m[:, :, tq - 1:tq - 1 + tk]
    tabs = []
    for t in range(3):
        valid = band & (kj >= 2 * ATT_TQ - ATT_TQ * t)
        tabs.append(jnp.where(valid[None], bias, NEG))
    return jnp.stack(tabs)


def _band_attention(proj, table, batch, seq, qcol, kcol, vcol):
    n = proj.shape[0]
    aw = A_HEADS * HEAD_DIM
    nq = seq // ATT_TQ

    def qmap(b, t):
        return (b * nq + t, qcol)

    def kvmap(col, back):
        return lambda b, t: (b * nq + jnp.maximum(t - back, 0), col)

    blk = (ATT_TQ, aw)
    return pl.pallas_call(
        _attn_kernel,
        grid=(batch, nq),
        in_specs=[pl.BlockSpec(blk, qmap),
                  pl.BlockSpec(blk, kvmap(kcol, 2)), pl.BlockSpec(blk, kvmap(kcol, 1)),
                  pl.BlockSpec(blk, kvmap(kcol, 0)),
                  pl.BlockSpec(blk, kvmap(vcol, 2)), pl.BlockSpec(blk, kvmap(vcol, 1)),
                  pl.BlockSpec(blk, kvmap(vcol, 0)),
                  pl.BlockSpec((1, A_HEADS, ATT_TQ, 3 * ATT_TQ),
                               lambda b, t: (jnp.minimum(t, 2), 0, 0, 0))],
        out_specs=pl.BlockSpec(blk, lambda b, t: (b * nq + t, 0)),
        out_shape=jax.ShapeDtypeStruct((n, aw), BF16),
        compiler_params=_cparams(("parallel", "arbitrary")),
        name="band_attention",
    )(proj, proj, proj, proj, proj, proj, proj, table)


def _lru_kernel(xg_ref, cw_ref, cb_ref, wbd_ref, bab_ref, lam_ref, o_ref,
                xpad_ref, hprev_ref, hbuf_ref, *, tt):
    w = LRU_WIDTH
    t = pl.program_id(1)

    @pl.when(t == 0)
    def _():
        xpad_ref[0:SUBLANES, :] = jnp.zeros((SUBLANES, w), F32)
        hprev_ref[...] = jnp.zeros((1, w), F32)

    xl = xg_ref[:, 0:w]
    gl = xg_ref[:, w:2 * w]
    xpad_ref[SUBLANES:SUBLANES + tt, :] = xl
    xc = cw_ref[3:4, :] * xl + cb_ref[...]
    for k in range(CONV_WIDTH - 1):
        off = SUBLANES - (CONV_WIDTH - 1) + k
        xc = xc + cw_ref[k:k + 1, :] * xpad_ref[off:off + tt, :]
    xpad_ref[0:SUBLANES, :] = xl[tt - SUBLANES:tt, :]

    z = jnp.dot(xc.astype(BF16), wbd_ref[...], preferred_element_type=F32) + bab_ref[...]
    r = _sigmoid(z[:, 0:w])
    i = _sigmoid(z[:, w:2 * w])
    nl = -lam_ref[...]
    softplus = jnp.maximum(nl, 0.0) + jnp.log(1.0 + jnp.exp(-jnp.abs(nl)))
    log_a = -LRU_C * r * softplus
    a = jnp.exp(log_a)
    y = jnp.clip(1.0 - a * a, 1e-12, 1.0)
    u = (y * lax.rsqrt(y)) * (i * xc)

    sub = lax.broadcasted_iota(jnp.int32, (tt, w), 0) % SUBLANES
    d = 1
    while d < SUBLANES:
        keep = sub >= d
        a_sh = jnp.where(keep, pltpu.roll(a, d, axis=0), 1.0)
        u_sh = jnp.where(keep, pltpu.roll(u, d, axis=0), 0.0)
        u = a * u_sh + u
        a = a * a_sh
        d *= 2
    carry = hprev_ref[...]
    for g in range(tt // SUBLANES):
        rows = slice(g * SUBLANES, (g + 1) * SUBLANES)
        h = u[rows] + a[rows] * carry
        carry = h[SUBLANES - 1:SUBLANES, :]
        hbuf_ref[rows, :] = h
    hprev_ref[...] = carry
    o_ref[...] = (hbuf_ref[...] * _gelu(gl)).astype(o_ref.dtype)


def _lru_branch(xg, conv_w, conv_b, wbd, bab, lam, batch, seq, tt=512):
    n = xg.shape[0]
    w = LRU_WIDTH
    nt = seq // tt
    const = lambda b, t: (0, 0)
    return pl.pallas_call(
        functools.partial(_lru_kernel, tt=tt),
        grid=(batch, nt),
        in_specs=[pl.BlockSpec((tt, 2 * w), lambda b, t: (b * nt + t, 0)),
                  pl.BlockSpec((CONV_WIDTH, w), const),
                  pl.BlockSpec((1, w), const),
                  pl.BlockSpec((w, 2 * w), const),
                  pl.BlockSpec((1, 2 * w), const),
                  pl.BlockSpec((1, w), const)],
        out_specs=pl.BlockSpec((tt, w), lambda b, t: (b * nt + t, 0)),
        out_shape=jax.ShapeDtypeStruct((n, w), BF16),
        scratch_shapes=[pltpu.VMEM((tt + SUBLANES, w), F32), pltpu.VMEM((1, w), F32),
                        pltpu.VMEM((tt, w), F32)],
        compiler_params=_cparams(("parallel", "arbitrary")),
        name="rg_lru",
    )(xg, conv_w, conv_b.reshape(1, w), wbd, bab, lam.reshape(1, w))


def _memkv_kernel(mem_ref, g_ref, w_ref, kg_ref, k_ref, v_ref):
    x = mem_ref[0]
    ms = jnp.mean(x * x, axis=-1, keepdims=True)
    mn = (x * lax.rsqrt(ms + EPS) * g_ref[...]).astype(BF16)
    kv = jnp.dot(mn, w_ref[...], preferred_element_type=F32)
    cw = C_HEADS * HEAD_DIM
    for h in range(C_HEADS):
        sl = slice(h * HEAD_DIM, (h + 1) * HEAD_DIM)
        blk = kv[:, sl]
        ms = jnp.mean(blk * blk, axis=-1, keepdims=True)
        k_ref[0, :, sl] = (blk * lax.rsqrt(ms + EPS) * kg_ref[...]).astype(k_ref.dtype)
    v_ref[0] = kv[:, cw:2 * cw].astype(v_ref.dtype)


def _mem_kv(mem, g, w_kv, kg):
    b, m, d = mem.shape
    cw = C_HEADS * HEAD_DIM
    out = jax.ShapeDtypeStruct((b, m, cw), BF16)
    return pl.pallas_call(
        _memkv_kernel,
        grid=(b,),
        in_specs=[pl.BlockSpec((1, m, d), lambda i: (i, 0, 0)),
                  pl.BlockSpec((1, d), lambda i: (0, 0)),
                  pl.BlockSpec((d, 2 * cw), lambda i: (0, 0)),
                  pl.BlockSpec((1, HEAD_DIM), lambda i: (0, 0))],
        out_specs=[pl.BlockSpec((1, m, cw), lambda i: (i, 0, 0)),
                   pl.BlockSpec((1, m, cw), lambda i: (i, 0, 0))],
        out_shape=[out, out],
        compiler_params=_cparams(("parallel",)),
        name="mem_kv",
    )(mem, g.reshape(1, d), w_kv, kg.reshape(1, HEAD_DIM))


def _merge_kernel(h_ref, oa_ref, ob_ref, qc_ref, km_ref, vm_ref, wg0_ref, wg1_ref, wg2_ref, bg_ref,
                  wba_ref, wbb_ref, wbc_ref, o_ref, oc_ref):
    j = pl.program_id(1)

    @pl.when(j == 0)
    def _():
        scale = HEAD_DIM ** -0.5
        nt = (((1,), (1,)), ((), ()))
        for hh in range(C_HEADS):
            sl = slice(hh * HEAD_DIM, (hh + 1) * HEAD_DIM)
            s = lax.dot_general(qc_ref[:, sl], km_ref[0, :, sl], nt, preferred_element_type=F32) * scale
            m = jnp.max(s, axis=-1, keepdims=True)
            p = jnp.exp(s - m)
            l = jnp.sum(p, axis=-1, keepdims=True)
            o = jnp.dot(p.astype(BF16), vm_ref[0, :, sl], preferred_element_type=F32)
            oc_ref[:, sl] = (o / l).astype(oc_ref.dtype)

    h = h_ref[...]
    merged = None
    for br, (o_br, wg_ref, wb_ref) in enumerate(((oa_ref[...], wg0_ref, wba_ref),
                                                 (ob_ref[...], wg1_ref, wbb_ref),
                                                 (oc_ref[...], wg2_ref, wbc_ref))):
        y = jnp.dot(o_br, wb_ref[...], preferred_element_type=F32)
        g = _sigmoid(jnp.dot(h, wg_ref[...], preferred_element_type=F32) + bg_ref[br:br + 1, :])
        merged = g * y if merged is None else merged + g * y
    o_ref[...] = merged.astype(o_ref.dtype)


def _merge(h, o_a, o_b, proj_a, qc_col, k_mem, v_mem, w_gate, b_gate3, w_branch, seq, tm=1024, tn=512):
    n, d = h.shape
    aw = A_HEADS * HEAD_DIM
    cw = C_HEADS * HEAD_DIM
    n_mem = k_mem.shape[1]
    ncol = d // tn
    tiles_per_seq = seq // tm
    return pl.pallas_call(
        _merge_kernel,
        grid=(n // tm, ncol),
        in_specs=[pl.BlockSpec((tm, d), lambda i, j: (i, 0)),
                  pl.BlockSpec((tm, aw), lambda i, j: (i, 0)),
                  pl.BlockSpec((tm, LRU_WIDTH), lambda i, j: (i, 0)),
                  pl.BlockSpec((tm, cw), lambda i, j: (i, qc_col)),
                  pl.BlockSpec((1, n_mem, cw), lambda i, j: (i // tiles_per_seq, 0, 0)),
                  pl.BlockSpec((1, n_mem, cw), lambda i, j: (i // tiles_per_seq, 0, 0)),
                  pl.BlockSpec((d, tn), lambda i, j: (0, j)),
                  pl.BlockSpec((d, tn), lambda i, j: (0, j + ncol)),
                  pl.BlockSpec((d, tn), lambda i, j: (0, j + 2 * ncol)),
                  pl.BlockSpec((3, tn), lambda i, j: (0, j)),
                  pl.BlockSpec((aw, tn), lambda i, j: (0, j)),
                  pl.BlockSpec((LRU_WIDTH, tn), lambda i, j: (aw // LRU_WIDTH, j)),
                  pl.BlockSpec((cw, tn), lambda i, j: ((aw + LRU_WIDTH) // cw, j))],
        out_specs=pl.BlockSpec((tm, tn), lambda i, j: (i, j)),
        out_shape=jax.ShapeDtypeStruct((n, d), BF16),
        scratch_shapes=[pltpu.VMEM((tm, cw), BF16)],
        compiler_params=_cparams(("parallel", "arbitrary")),
        name="gated_merge",
    )(h, o_a, o_b, proj_a, k_mem, v_mem, w_gate, w_gate, w_gate, b_gate3, w_branch, w_branch, w_branch)


def _oproj_kernel(x_ref, m_ref, w_ref, g_ref, x1_ref, h2t_ref):
    x1 = x_ref[...] + jnp.dot(m_ref[...], w_ref[...], preferred_element_type=F32)
    x1_ref[...] = x1
    ms = jnp.mean(x1 * x1, axis=-1, keepdims=True)
    h2 = x1 * lax.rsqrt(ms + EPS) * g_ref[...]
    h2t_ref[...] = h2.T.astype(h2t_ref.dtype)


def _out_proj(x, merged, w_o, g2, tm=512):
    n, d = x.shape
    return pl.pallas_call(
        _oproj_kernel,
        grid=(n // tm,),
        in_specs=[pl.BlockSpec((tm, d), lambda i: (i, 0)),
                  pl.BlockSpec((tm, d), lambda i: (i, 0)),
                  pl.BlockSpec((d, d), lambda i: (0, 0)),
                  pl.BlockSpec((1, d), lambda i: (0, 0))],
        out_specs=[pl.BlockSpec((tm, d), lambda i: (i, 0)),
                   pl.BlockSpec((d, tm), lambda i: (0, i))],
        out_shape=[jax.ShapeDtypeStruct((n, d), F32), jax.ShapeDtypeStruct((d, n), BF16)],
        compiler_params=_cparams(("parallel",)),
        name="out_proj_norm2",
    )(x, merged, w_o, g2.reshape(1, d))


def _cmpx(v, i, j):
    a, b = v[i], v[j]
    if b is None:
        return
    if a is None:
        v[i], v[j] = b, None
        return
    v[i], v[j] = jnp.maximum(a, b), jnp.minimum(a, b)


def _bitonic_merge_desc(v, lo, n):
    j = n // 2
    while j >= 1:
        for i in range(lo, lo + n):
            if (i - lo) & j == 0:
                _cmpx(v, i, i + j)
        j //= 2


def _bitonic_sort_desc(v):
    n = len(v)
    k = 2
    while k <= n:
        j = k // 2
        while j >= 1:
            for i in range(n):
                l = i ^ j
                if l > i:
                    if i & k == 0 or k == n:
                        _cmpx(v, i, l)
                    else:
                        _cmpx(v, l, i)
            j //= 2
        k *= 2


def _top17_desc(x):
    k = PEER_TOPK
    groups = []
    for g in range(x.shape[0] // k):
        run = [x[g * k + r] for r in range(k)]
        _bitonic_sort_desc(run)
        groups.append(run)
    dropped = None
    while len(groups) > 1:
        merged = []
        for g in range(0, len(groups), 2):
            p, q = groups[g], groups[g + 1]
            hi = [jnp.maximum(p[r], q[k - 1 - r]) for r in range(k)]
            lo = functools.reduce(jnp.maximum, [jnp.minimum(p[r], q[k - 1 - r]) for r in range(k)])
            dropped = lo if dropped is None else jnp.maximum(dropped, lo)
            _bitonic_merge_desc(hi, 0, k)
            merged.append(hi)
        groups = merged
    return groups[0] + [dropped]


def _route_kernel(ht_ref, wq_ref, kb1_ref, kb2_ref, s1_ref, s2h_ref, st_ref,
                  s2_ref, s2c_ref, *, tm):
    nk = N_KEYS * PEER_HEADS
    qt = jnp.dot(wq_ref[...], ht_ref[...], preferred_element_type=F32)
    s1_ref[...] = jnp.dot(kb1_ref[...], qt[0:nk].astype(BF16), preferred_element_type=F32)
    s2_ref[...] = jnp.dot(kb2_ref[...], qt[nk:2 * nk].astype(BF16), preferred_element_type=F32)

    def chunk(tb, carry):
        for sub in range(ROUTE_LANES // LANES):
            lsl = pl.ds(pl.multiple_of(tb * ROUTE_LANES + sub * LANES, LANES), LANES)
            s2c_ref[...] = s2_ref[:, lsl]
            for h in range(PEER_HEADS):
                s2h_ref[h, :, lsl] = s2c_ref[pl.ds(h, N_KEYS, stride=PEER_HEADS), :]
        tsl = pl.ds(pl.multiple_of(tb * ROUTE_LANES, ROUTE_LANES), ROUTE_LANES)
        a = _top17_desc(s1_ref[:, tsl].reshape(N_KEYS, PEER_HEADS, ROUTE_LANES))
        b = _top17_desc(s2_ref[:, tsl].reshape(N_KEYS, PEER_HEADS, ROUTE_LANES))
        nt = PEER_TOPK + 1
        cands = [a[p] + b[q] for p in range(nt) for q in range(nt) if (p + 1) * (q + 1) <= nt]
        order = cands + [None] * (64 - len(cands))
        _bitonic_sort_desc(order)
        thr = 0.5 * (order[PEER_TOPK - 1] + order[PEER_TOPK])
        m = order[0]
        z = functools.reduce(
            jnp.add, [jnp.where(cv >= thr, jnp.exp(cv - m), 0.0) for cv in cands])
        st_ref[0, :, tsl] = thr
        st_ref[1, :, tsl] = a[0]
        st_ref[2, :, tsl] = b[0]
        st_ref[3, :, tsl] = 1.0 / z
        return carry

    lax.fori_loop(0, tm // ROUTE_LANES, chunk, 0)


def _peer_route(h2t, wq_t, kb1, kb2, tm=512):
    d, n = h2t.shape
    nk = N_KEYS * PEER_HEADS
    s_shape = jax.ShapeDtypeStruct((nk, n), F32)
    return pl.pallas_call(
        functools.partial(_route_kernel, tm=tm),
        grid=(n // tm,),
        in_specs=[pl.BlockSpec((d, tm), lambda i: (0, i)),
                  pl.BlockSpec((2 * nk, d), lambda i: (0, 0)),
                  pl.BlockSpec((nk, nk), lambda i: (0, 0)),
                  pl.BlockSpec((nk, nk), lambda i: (0, 0))],
        out_specs=[pl.BlockSpec((nk, tm), lambda i: (0, i)),
                   pl.BlockSpec((PEER_HEADS, N_KEYS, tm), lambda i: (0, 0, i)),
                   pl.BlockSpec((4, PEER_HEADS, tm), lambda i: (0, 0, i))],
        out_shape=[s_shape, jax.ShapeDtypeStruct((PEER_HEADS, N_KEYS, n), F32),
                   jax.ShapeDtypeStruct((4, PEER_HEADS, n), F32)],
        scratch_shapes=[pltpu.VMEM((nk, tm), F32),
                        pltpu.VMEM((nk, LANES), F32)],
        compiler_params=_cparams(("parallel",)),
        name="peer_route",
    )(h2t, wq_t, kb1, kb2)


def _peer_kernel(ht_ref, u_ref, vt_ref, s1_ref, s2h_ref, st_ref, acc_ref,
                 e2_ref, c1_ref, th_ref, z_ref, g_ref, w_ref, *, tm, te, ne, n_steps):
    s = pl.program_id(0)
    e1 = jnp.minimum(s, n_steps - 1) % ne
    prev = jnp.maximum(s - 1, 0)
    ni = te // N_KEYS
    slot = s % 2

    @pl.when(s == 0)
    def _():
        w_ref[...] = jnp.zeros(w_ref.shape, w_ref.dtype)

    @pl.when(prev % ne == 0)
    def _():
        acc_ref[...] = jnp.zeros(acc_ref.shape, F32)

    @pl.when(e1 == 0)
    def _():
        for h in range(PEER_HEADS):
            e2_ref[h] = jnp.exp(s2h_ref[h] - st_ref[2, h:h + 1, :])
        c1_ref[...] = jnp.exp(s1_ref[...] - st_ref[1][None]) * st_ref[3][None]
        th_ref[...] = st_ref[0][None] - s1_ref[...]

    for i in range(ni):
        key1 = e1 * ni + i
        for tb in range(tm // LANES):
            tsl = slice(tb * LANES, (tb + 1) * LANES)
            for jb in range(N_KEYS // PEER_GATE_ROWS):
                jr = slice(jb * PEER_GATE_ROWS, (jb + 1) * PEER_GATE_ROWS)
                gate = None
                for h in range(PEER_HEADS):
                    hit = s2h_ref[h, jr, tsl] >= th_ref[key1, h:h + 1, tsl]
                    term = jnp.where(hit, e2_ref[h, jr, tsl] * c1_ref[key1, h:h + 1, tsl], 0.0)
                    gate = term if gate is None else gate + term
                g_ref[i * N_KEYS + jb * PEER_GATE_ROWS:i * N_KEYS + (jb + 1) * PEER_GATE_ROWS, tsl] = gate

    z_ref[...] = jnp.dot(u_ref[...], ht_ref[...], preferred_element_type=F32)
    acc_ref[...] += jnp.dot(vt_ref[...], w_ref[1 - slot], preferred_element_type=F32)

    w_ref[slot] = (_gelu_sigmoid_form(z_ref[...]) * g_ref[...]).astype(w_ref.dtype)


def _peer_dense(h2t, u, vt, s1, s2h, st, tm=512, te=1024):
    d, n = h2t.shape
    ne = u.shape[0] // te
    n_steps = (n // tm) * ne

    def cur(s):
        return jnp.minimum(s, n_steps - 1)

    def prev(s):
        return jnp.maximum(s - 1, 0)

    return pl.pallas_call(
        functools.partial(_peer_kernel, tm=tm, te=te, ne=ne, n_steps=n_steps),
        grid=(n_steps + 1,),
        in_specs=[pl.BlockSpec((d, tm), lambda s: (0, cur(s) // ne)),
                  pl.BlockSpec((te, d), lambda s: (cur(s) % ne, 0)),
                  pl.BlockSpec((d, te), lambda s: (0, prev(s) % ne)),
                  pl.BlockSpec((N_KEYS, PEER_HEADS, tm), lambda s: (0, 0, cur(s) // ne)),
                  pl.BlockSpec((PEER_HEADS, N_KEYS, tm), lambda s: (0, 0, cur(s) // ne)),
                  pl.BlockSpec((4, PEER_HEADS, tm), lambda s: (0, 0, cur(s) // ne))],
        out_specs=pl.BlockSpec((d, tm), lambda s: (0, prev(s) // ne)),
        out_shape=jax.ShapeDtypeStruct((d, n), F32),
        scratch_shapes=[pltpu.VMEM((PEER_HEADS, N_KEYS, tm), F32),
                        pltpu.VMEM((N_KEYS, PEER_HEADS, tm), F32),
                        pltpu.VMEM((N_KEYS, PEER_HEADS, tm), F32),
                        pltpu.VMEM((te, tm), F32),
                        pltpu.VMEM((te, tm), F32),
                        pltpu.VMEM((2, te, tm), BF16)],
        compiler_params=_cparams(("arbitrary",)),
        name="peer_dense",
    )(h2t, u, vt, s1, s2h, st)


def _final_kernel(x1_ref, pt_ref, o_ref):
    o_ref[...] = x1_ref[...] + pt_ref[...].T


def _final_add(x1, peer_t, tm=512):
    n, d = x1.shape
    return pl.pallas_call(
        _final_kernel,
        grid=(n // tm,),
        in_specs=[pl.BlockSpec((tm, d), lambda i: (i, 0)),
                  pl.BlockSpec((d, tm), lambda i: (0, i))],
        out_specs=pl.BlockSpec((tm, d), lambda i: (i, 0)),
        out_shape=jax.ShapeDtypeStruct((n, d), F32),
        compiler_params=_cparams(("parallel",)),
        name="final_residual",
    )(x1, peer_t)


def _block_diag(w):
    g, a, b = w.shape
    eye = jnp.eye(g, dtype=w.dtype)
    return jnp.einsum("gab,gh->gahb", w, eye).reshape(g * a, g * b)


def _key_matrix(keys_c):
    h, k, dh = keys_c.shape
    eye = jnp.eye(h, dtype=keys_c.dtype)
    return jnp.einsum("hkd,hg->khgd", keys_c, eye).reshape(k * h, h * dh)


def kernel(x, mem, norm1_g, w_in, attn_q_norm_g, attn_k_norm_g, rel_bias, conv_w, conv_b, lru_wa, lru_ba, lru_wx, lru_bx, lru_lambda, mem_norm_g, w_mem_kv, xq_norm_g, xk_norm_g, w_gate, b_gate, w_branch, w_o, norm2_g, w_peer_q, peer_keys, peer_u, peer_v):
    batch, seq, d = x.shape
    n = batch * seq
    aw = A_HEADS * HEAD_DIM
    cw = C_HEADS * HEAD_DIM
    lw = LRU_WIDTH
    x2 = x.reshape(n, d)

    w_in_b = w_in.astype(BF16)
    tn = 512
    lru_tile0, qc_tile0 = 3 * aw // tn, (3 * aw + 2 * lw) // tn
    gains_a = jnp.concatenate([jnp.tile(attn_q_norm_g * HEAD_DIM ** -0.5, A_HEADS),
                               jnp.tile(attn_k_norm_g, A_HEADS),
                               jnp.ones((aw,), F32), jnp.tile(xq_norm_g, C_HEADS)]).reshape(1, -1)
    wbd = jnp.concatenate([_block_diag(lru_wa), _block_diag(lru_wx)], axis=1).astype(BF16)
    bab = jnp.concatenate([lru_ba, lru_bx]).reshape(1, 2 * lw)
    table = _attn_bias_table(rel_bias)
    hd = PEER_HEADS * N_KEYS
    wq_t = w_peer_q.reshape(d, PEER_HEADS, 2, HEAD_DIM).transpose(2, 1, 3, 0).reshape(2 * hd, d).astype(BF16)
    kb1 = _key_matrix(peer_keys[:, 0]).astype(BF16)
    kb2 = _key_matrix(peer_keys[:, 1]).astype(BF16)
    u_b = peer_u.astype(BF16)
    vt_b = peer_v.T.astype(BF16)

    h = _rmsnorm(x2, norm1_g)
    proj_a = _proj(h, w_in_b, lambda j: jnp.where(j < lru_tile0, j, j + (qc_tile0 - lru_tile0)),
                   gains_a, (2 * aw, 3 * aw), BF16, "in_proj_qkv", tn=tn)
    xg = _proj(h, w_in_b, lambda j: j + lru_tile0, jnp.ones((1, 2 * lw), F32),
               (0, 2 * lw), F32, "in_proj_lru", tn=tn)
    o_a = _band_attention(proj_a, table, batch, seq, qcol=0, kcol=1, vcol=2)
    o_b = _lru_branch(xg, conv_w, conv_b, wbd, bab, lru_lambda, batch, seq)
    k_mem, v_mem = _mem_kv(mem, mem_norm_g, w_mem_kv.astype(BF16), xk_norm_g)
    merged = _merge(h, o_a, o_b, proj_a, (3 * aw) // cw, k_mem, v_mem, w_gate.astype(BF16),
                    b_gate.reshape(3, d), w_branch.astype(BF16), seq)
    x1, h2t = _out_proj(x2, merged, w_o.astype(BF16), norm2_g)

    s1, s2h, st = _peer_route(h2t, wq_t, kb1, kb2)
    peer_t = _peer_dense(h2t, u_b, vt_b, s1.reshape(N_KEYS, PEER_HEADS, n), s2h, st)
    out = _final_add(x1, peer_t)
    return out.reshape(batch, seq, d)
```

```python
import functools

import numpy as np
import jax
import jax.numpy as jnp
from jax import lax
from jax.experimental import pallas as pl
from jax.experimental.pallas import tpu as pltpu

F32 = jnp.float32
BF16 = jnp.bfloat16

EPS = 1e-6
NEG = -1e30
LANES = 128
SUBLANES = 8
VMEM_LIMIT = 56 * 1024 * 1024

CHUNK = 64
N_PREV_CHUNKS = 8
REL_CLIP = 128
A_HEADS = 8
HEAD_DIM = 128
C_HEADS = 4
LRU_WIDTH = 512
LRU_C = 8.0
CONV_WIDTH = 4
PEER_HEADS = 8
N_KEYS = 128
PEER_TOPK = 16

PEER_GATE_ROWS = 64
ROUTE_LANES = 256
ATT_TQ = 256
ATT_TK = ATT_TQ + N_PREV_CHUNKS * CHUNK


def _cparams(sem):
    return pltpu.CompilerParams(dimension_semantics=sem, vmem_limit_bytes=VMEM_LIMIT)


def _gelu(x):
    return 0.5 * x * (1.0 + jnp.tanh(0.7978845608028654 * (x + 0.044715 * (x * x * x))))


_GELU_K1 = -2.0 * 0.7978845608028654 * 1.4426950408889634
_GELU_K2 = _GELU_K1 * 0.044715


def _gelu_sigmoid_form(x):
    return x / (1.0 + jnp.exp2(x * (_GELU_K1 + _GELU_K2 * (x * x))))


def _sigmoid(x):
    return 1.0 / (1.0 + jnp.exp(-x))


def _rmsnorm_kernel(x_ref, g_ref, o_ref):
    x = x_ref[...]
    ms = jnp.mean(x * x, axis=-1, keepdims=True)
    o_ref[...] = (x * lax.rsqrt(ms + EPS) * g_ref[...]).astype(o_ref.dtype)


def _rmsnorm(x, g, tr=512):
    n, d = x.shape
    return pl.pallas_call(
        _rmsnorm_kernel,
        grid=(n // tr,),
        in_specs=[pl.BlockSpec((tr, d), lambda i: (i, 0)),
                  pl.BlockSpec((1, d), lambda i: (0, 0))],
        out_specs=pl.BlockSpec((tr, d), lambda i: (i, 0)),
        out_shape=jax.ShapeDtypeStruct((n, d), BF16),
        compiler_params=_cparams(("parallel",)),
        name="rmsnorm1",
    )(x, g.reshape(1, d))


def _proj_kernel(h_ref, w_ref, g_ref, o_ref, *, plain_lo, plain_hi, tn):
    acc = jnp.dot(h_ref[...], w_ref[...], preferred_element_type=F32)
    j = pl.program_id(1)
    plain = (j >= plain_lo) & (j < plain_hi)

    @pl.when(jnp.logical_not(plain))
    def _():
        for c in range(tn // HEAD_DIM):
            sl = slice(c * HEAD_DIM, (c + 1) * HEAD_DIM)
            blk = acc[:, sl]
            ms = jnp.mean(blk * blk, axis=-1, keepdims=True)
            o_ref[:, sl] = (blk * lax.rsqrt(ms + EPS) * g_ref[:, sl]).astype(o_ref.dtype)

    @pl.when(plain)
    def _():
        o_ref[...] = acc.astype(o_ref.dtype)


def _proj(h, w, w_col_tile, gains, plain_cols, out_dtype, name, tm=1024, tn=512):
    n, d = h.shape
    nc = gains.shape[1]
    kern = functools.partial(_proj_kernel, plain_lo=plain_cols[0] // tn, plain_hi=plain_cols[1] // tn, tn=tn)
    return pl.pallas_call(
        kern,
        grid=(n // tm, nc // tn),
        in_specs=[pl.BlockSpec((tm, d), lambda i, j: (i, 0)),
                  pl.BlockSpec((d, tn), lambda i, j: (0, w_col_tile(j))),
                  pl.BlockSpec((1, tn), lambda i, j: (0, j))],
        out_specs=pl.BlockSpec((tm, tn), lambda i, j: (i, j)),
        out_shape=jax.ShapeDtypeStruct((n, nc), out_dtype),
        compiler_params=_cparams(("parallel", "arbitrary")),
        name=name,
    )(h, w, gains)


def _attn_kernel(q_ref, k0_ref, k1_ref, k2_ref, v0_ref, v1_ref, v2_ref, t_ref, o_ref):
    nt = (((1,), (1,)), ((), ()))
    for h in range(A_HEADS):
        sl = slice(h * HEAD_DIM, (h + 1) * HEAD_DIM)
        q = q_ref[:, sl]
        s = jnp.concatenate(
            [lax.dot_general(q, kr[:, sl], nt, preferred_element_type=F32)
             for kr in (k0_ref, k1_ref, k2_ref)], axis=1)
        s = s + t_ref[0, h]
        m = jnp.max(s, axis=-1, keepdims=True)
        p = jnp.exp(s - m)
        l = jnp.sum(p, axis=-1, keepdims=True)
        pb = p.astype(BF16)
        o = jnp.dot(pb[:, 0:ATT_TQ], v0_ref[:, sl], preferred_element_type=F32)
        o += jnp.dot(pb[:, ATT_TQ:2 * ATT_TQ], v1_ref[:, sl], preferred_element_type=F32)
        o += jnp.dot(pb[:, 2 * ATT_TQ:], v2_ref[:, sl], preferred_element_type=F32)
        o_ref[:, sl] = (o / l).astype(o_ref.dtype)


def _attn_bias_table(rel_bias):
    nh = rel_bias.shape[0]
    tq, tk = ATT_TQ, 3 * ATT_TQ
    qi = np.arange(tq)[:, None]
    kj = np.arange(tk)[None, :]
    qch = qi // CHUNK
    kch = kj // CHUNK - (2 * tq) // CHUNK
    band = (kch <= qch) & (kch >= qch - N_PREV_CHUNKS)
    n_far = 3 * tq - 1 - REL_CLIP + 1
    n_near = tq + tk - 1 - n_far - (2 * REL_CLIP - 1)
    g = jnp.concatenate([jnp.broadcast_to(rel_bias[:, 2 * REL_CLIP:], (nh, n_far)),
                         rel_bias[:, 1:2 * REL_CLIP][:, ::-1],
                         jnp.broadcast_to(rel_bias[:, :1], (nh, n_near))], axis=1).astype(F32)
    ln = tq + tk
    gp = jnp.pad(g, ((0, 0), (0, 1)))
    m = jnp.tile(gp, (1, tq))[:, :tq * (ln - 1)].reshape(nh, tq, ln - 1)
    bias = m[:, :, tq - 1:tq - 1 + tk]
    tabs = []
    for t in range(3):
        valid = band & (kj >= 2 * ATT_TQ - ATT_TQ * t)
        tabs.append(jnp.where(valid[None], bias, NEG))
    return jnp.stack(tabs)


def _band_attention(proj, table, batch, seq, qcol, kcol, vcol):
    n = proj.shape[0]
    aw = A_HEADS * HEAD_DIM
    nq = seq // ATT_TQ

    def qmap(b, t):
        return (b * nq + t, qcol)

    def kvmap(col, back):
        return lambda b, t: (b * nq + jnp.maximum(t - back, 0), col)

    blk = (ATT_TQ, aw)
    return pl.pallas_call(
        _attn_kernel,
        grid=(batch, nq),
        in_specs=[pl.BlockSpec(blk, qmap),
                  pl.BlockSpec(blk, kvmap(kcol, 2)), pl.BlockSpec(blk, kvmap(kcol, 1)),
                  pl.BlockSpec(blk, kvmap(kcol, 0)),
                  pl.BlockSpec(blk, kvmap(vcol, 2)), pl.BlockSpec(blk, kvmap(vcol, 1)),
                  pl.BlockSpec(blk, kvmap(vcol, 0)),
                  pl.BlockSpec((1, A_HEADS, ATT_TQ, 3 * ATT_TQ),
                               lambda b, t: (jnp.minimum(t, 2), 0, 0, 0))],
        out_specs=pl.BlockSpec(blk, lambda b, t: (b * nq + t, 0)),
        out_shape=jax.ShapeDtypeStruct((n, aw), BF16),
        compiler_params=_cparams(("parallel", "arbitrary")),
        name="band_attention",
    )(proj, proj, proj, proj, proj, proj, proj, table)


def _lru_kernel(xg_ref, cw_ref, cb_ref, wbd_ref, bab_ref, lam_ref, o_ref,
                xpad_ref, hprev_ref, hbuf_ref, *, tt):
    w = LRU_WIDTH
    t = pl.program_id(1)

    @pl.when(t == 0)
    def _():
        xpad_ref[0:SUBLANES, :] = jnp.zeros((SUBLANES, w), F32)
        hprev_ref[...] = jnp.zeros((1, w), F32)

    xl = xg_ref[:, 0:w]
    gl = xg_ref[:, w:2 * w]
    xpad_ref[SUBLANES:SUBLANES + tt, :] = xl
    xc = cw_ref[3:4, :] * xl + cb_ref[...]
    for k in range(CONV_WIDTH - 1):
        off = SUBLANES - (CONV_WIDTH - 1) + k
        xc = xc + cw_ref[k:k + 1, :] * xpad_ref[off:off + tt, :]
    xpad_ref[0:SUBLANES, :] = xl[tt - SUBLANES:tt, :]

    z = jnp.dot(xc.astype(BF16), wbd_ref[...], preferred_element_type=F32) + bab_ref[...]
    r = _sigmoid(z[:, 0:w])
    i = _sigmoid(z[:, w:2 * w])
    nl = -lam_ref[...]
    softplus = jnp.maximum(nl, 0.0) + jnp.log(1.0 + jnp.exp(-jnp.abs(nl)))
    log_a = -LRU_C * r * softplus
    a = jnp.exp(log_a)
    y = jnp.clip(1.0 - a * a, 1e-12, 1.0)
    u = (y * lax.rsqrt(y)) * (i * xc)

    sub = lax.broadcasted_iota(jnp.int32, (tt, w), 0) % SUBLANES
    d = 1
    while d < SUBLANES:
        keep = sub >= d
        a_sh = jnp.where(keep, pltpu.roll(a, d, axis=0), 1.0)
        u_sh = jnp.where(keep, pltpu.roll(u, d, axis=0), 0.0)
        u = a * u_sh + u
        a = a * a_sh
        d *= 2
    carry = hprev_ref[...]
    for g in range(tt // SUBLANES):
        rows = slice(g * SUBLANES, (g + 1) * SUBLANES)
        h = u[rows] + a[rows] * carry
        carry = h[SUBLANES - 1:SUBLANES, :]
        hbuf_ref[rows, :] = h
    hprev_ref[...] = carry
    o_ref[...] = (hbuf_ref[...] * _gelu(gl)).astype(o_ref.dtype)


def _lru_branch(xg, conv_w, conv_b, wbd, bab, lam, batch, seq, tt=512):
    n = xg.shape[0]
    w = LRU_WIDTH
    nt = seq // tt
    const = lambda b, t: (0, 0)
    return pl.pallas_call(
        functools.partial(_lru_kernel, tt=tt),
        grid=(batch, nt),
        in_specs=[pl.BlockSpec((tt, 2 * w), lambda b, t: (b * nt + t, 0)),
                  pl.BlockSpec((CONV_WIDTH, w), const),
                  pl.BlockSpec((1, w), const),
                  pl.BlockSpec((w, 2 * w), const),
                  pl.BlockSpec((1, 2 * w), const),
                  pl.BlockSpec((1, w), const)],
        out_specs=pl.BlockSpec((tt, w), lambda b, t: (b * nt + t, 0)),
        out_shape=jax.ShapeDtypeStruct((n, w), BF16),
        scratch_shapes=[pltpu.VMEM((tt + SUBLANES, w), F32), pltpu.VMEM((1, w), F32),
                        pltpu.VMEM((tt, w), F32)],
        compiler_params=_cparams(("parallel", "arbitrary")),
        name="rg_lru",
    )(xg, conv_w, conv_b.reshape(1, w), wbd, bab, lam.reshape(1, w))


def _memkv_kernel(mem_ref, g_ref, w_ref, kg_ref, k_ref, v_ref):
    x = mem_ref[0]
    ms = jnp.mean(x * x, axis=-1, keepdims=True)
    mn = (x * lax.rsqrt(ms + EPS) * g_ref[...]).astype(BF16)
    kv = jnp.dot(mn, w_ref[...], preferred_element_type=F32)
    cw = C_HEADS * HEAD_DIM
    for h in range(C_HEADS):
        sl = slice(h * HEAD_DIM, (h + 1) * HEAD_DIM)
        blk = kv[:, sl]
        ms = jnp.mean(blk * blk, axis=-1, keepdims=True)
        k_ref[0, :, sl] = (blk * lax.rsqrt(ms + EPS) * kg_ref[...]).astype(k_ref.dtype)
    v_ref[0] = kv[:, cw:2 * cw].astype(v_ref.dtype)


def _mem_kv(mem, g, w_kv, kg):
    b, m, d = mem.shape
    cw = C_HEADS * HEAD_DIM
    out = jax.ShapeDtypeStruct((b, m, cw), BF16)
    return pl.pallas_call(
        _memkv_kernel,
        grid=(b,),
        in_specs=[pl.BlockSpec((1, m, d), lambda i: (i, 0, 0)),
                  pl.BlockSpec((1, d), lambda i: (0, 0)),
                  pl.BlockSpec((d, 2 * cw), lambda i: (0, 0)),
                  pl.BlockSpec((1, HEAD_DIM), lambda i: (0, 0))],
        out_specs=[pl.BlockSpec((1, m, cw), lambda i: (i, 0, 0)),
                   pl.BlockSpec((1, m, cw), lambda i: (i, 0, 0))],
        out_shape=[out, out],
        compiler_params=_cparams(("parallel",)),
        name="mem_kv",
    )(mem, g.reshape(1, d), w_kv, kg.reshape(1, HEAD_DIM))


def _merge_kernel(h_ref, oa_ref, ob_ref, qc_ref, km_ref, vm_ref, wg0_ref, wg1_ref, wg2_ref, bg_ref,
                  wba_ref, wbb_ref, wbc_ref, o_ref, oc_ref):
    j = pl.program_id(1)

    @pl.when(j == 0)
    def _():
        scale = HEAD_DIM ** -0.5
        nt = (((1,), (1,)), ((), ()))
        for hh in range(C_HEADS):
            sl = slice(hh * HEAD_DIM, (hh + 1) * HEAD_DIM)
            s = lax.dot_general(qc_ref[:, sl], km_ref[0, :, sl], nt, preferred_element_type=F32) * scale
            m = jnp.max(s, axis=-1, keepdims=True)
            p = jnp.exp(s - m)
            l = jnp.sum(p, axis=-1, keepdims=True)
            o = jnp.dot(p.astype(BF16), vm_ref[0, :, sl], preferred_element_type=F32)
            oc_ref[:, sl] = (o / l).astype(oc_ref.dtype)

    h = h_ref[...]
    merged = None
    for br, (o_br, wg_ref, wb_ref) in enumerate(((oa_ref[...], wg0_ref, wba_ref),
                                                 (ob_ref[...], wg1_ref, wbb_ref),
                                                 (oc_ref[...], wg2_ref, wbc_ref))):
        y = jnp.dot(o_br, wb_ref[...], preferred_element_type=F32)
        g = _sigmoid(jnp.dot(h, wg_ref[...], preferred_element_type=F32) + bg_ref[br:br + 1, :])
        merged = g * y if merged is None else merged + g * y
    o_ref[...] = merged.astype(o_ref.dtype)


def _merge(h, o_a, o_b, proj_a, qc_col, k_mem, v_mem, w_gate, b_gate3, w_branch, seq, tm=1024, tn=512):
    n, d = h.shape
    aw = A_HEADS * HEAD_DIM
    cw = C_HEADS * HEAD_DIM
    n_mem = k_mem.shape[1]
    ncol = d // tn
    tiles_per_seq = seq // tm
    return pl.pallas_call(
        _merge_kernel,
        grid=(n // tm, ncol),
        in_specs=[pl.BlockSpec((tm, d), lambda i, j: (i, 0)),
                  pl.BlockSpec((tm, aw), lambda i, j: (i, 0)),
                  pl.BlockSpec((tm, LRU_WIDTH), lambda i, j: (i, 0)),
                  pl.BlockSpec((tm, cw), lambda i, j: (i, qc_col)),
                  pl.BlockSpec((1, n_mem, cw), lambda i, j: (i // tiles_per_seq, 0, 0)),
                  pl.BlockSpec((1, n_mem, cw), lambda i, j: (i // tiles_per_seq, 0, 0)),
                  pl.BlockSpec((d, tn), lambda i, j: (0, j)),
                  pl.BlockSpec((d, tn), lambda i, j: (0, j + ncol)),
                  pl.BlockSpec((d, tn), lambda i, j: (0, j + 2 * ncol)),
                  pl.BlockSpec((3, tn), lambda i, j: (0, j)),
                  pl.BlockSpec((aw, tn), lambda i, j: (0, j)),
                  pl.BlockSpec((LRU_WIDTH, tn), lambda i, j: (aw // LRU_WIDTH, j)),
                  pl.BlockSpec((cw, tn), lambda i, j: ((aw + LRU_WIDTH) // cw, j))],
        out_specs=pl.BlockSpec((tm, tn), lambda i, j: (i, j)),
        out_shape=jax.ShapeDtypeStruct((n, d), BF16),
        scratch_shapes=[pltpu.VMEM((tm, cw), BF16)],
        compiler_params=_cparams(("parallel", "arbitrary")),
        name="gated_merge",
    )(h, o_a, o_b, proj_a, k_mem, v_mem, w_gate, w_gate, w_gate, b_gate3, w_branch, w_branch, w_branch)


def _oproj_kernel(x_ref, m_ref, w_ref, g_ref, x1_ref, h2t_ref):
    x1 = x_ref[...] + jnp.dot(m_ref[...], w_ref[...], preferred_element_type=F32)
    x1_ref[...] = x1
    ms = jnp.mean(x1 * x1, axis=-1, keepdims=True)
    h2 = x1 * lax.rsqrt(ms + EPS) * g_ref[...]
    h2t_ref[...] = h2.T.astype(h2t_ref.dtype)


def _out_proj(x, merged, w_o, g2, tm=512):
    n, d = x.shape
    return pl.pallas_call(
        _oproj_kernel,
        grid=(n // tm,),
        in_specs=[pl.BlockSpec((tm, d), lambda i: (i, 0)),
                  pl.BlockSpec((tm, d), lambda i: (i, 0)),
                  pl.BlockSpec((d, d), lambda i: (0, 0)),
                  pl.BlockSpec((1, d), lambda i: (0, 0))],
        out_specs=[pl.BlockSpec((tm, d), lambda i: (i, 0)),
                   pl.BlockSpec((d, tm), lambda i: (0, i))],
        out_shape=[jax.ShapeDtypeStruct((n, d), F32), jax.ShapeDtypeStruct((d, n), BF16)],
        compiler_params=_cparams(("parallel",)),
        name="out_proj_norm2",
    )(x, merged, w_o, g2.reshape(1, d))


def _cmpx(v, i, j):
    a, b = v[i], v[j]
    if b is None:
        return
    if a is None:
        v[i], v[j] = b, None
        return
    v[i], v[j] = jnp.maximum(a, b), jnp.minimum(a, b)


def _bitonic_merge_desc(v, lo, n):
    j = n // 2
    while j >= 1:
        for i in range(lo, lo + n):
            if (i - lo) & j == 0:
                _cmpx(v, i, i + j)
        j //= 2


def _bitonic_sort_desc(v):
    n = len(v)
    k = 2
    while k <= n:
        j = k // 2
        while j >= 1:
            for i in range(n):
                l = i ^ j
                if l > i:
                    if i & k == 0 or k == n:
                        _cmpx(v, i, l)
                    else:
                        _cmpx(v, l, i)
            j //= 2
        k *= 2


def _top17_desc(x):
    k = PEER_TOPK
    groups = []
    for g in range(x.shape[0] // k):
        run = [x[g * k + r] for r in range(k)]
        _bitonic_sort_desc(run)
        groups.append(run)
    dropped = None
    while len(groups) > 1:
        merged = []
        for g in range(0, len(groups), 2):
            p, q = groups[g], groups[g + 1]
            hi = [jnp.maximum(p[r], q[k - 1 - r]) for r in range(k)]
            lo = functools.reduce(jnp.maximum, [jnp.minimum(p[r], q[k - 1 - r]) for r in range(k)])
            dropped = lo if dropped is None else jnp.maximum(dropped, lo)
            _bitonic_merge_desc(hi, 0, k)
            merged.append(hi)
        groups = merged
    return groups[0] + [dropped]


def _route_kernel(ht_ref, wq_ref, kb1_ref, kb2_ref, s1_ref, s2h_ref, st_ref,
                  s2_ref, s2c_ref, *, tm):
    nk = N_KEYS * PEER_HEADS
    qt = jnp.dot(wq_ref[...], ht_ref[...], preferred_element_type=F32)
    s1_ref[...] = jnp.dot(kb1_ref[...], qt[0:nk].astype(BF16), preferred_element_type=F32)
    s2_ref[...] = jnp.dot(kb2_ref[...], qt[nk:2 * nk].astype(BF16), preferred_element_type=F32)

    def chunk(tb, carry):
        for sub in range(ROUTE_LANES // LANES):
            lsl = pl.ds(pl.multiple_of(tb * ROUTE_LANES + sub * LANES, LANES), LANES)
            s2c_ref[...] = s2_ref[:, lsl]
            for h in range(PEER_HEADS):
                s2h_ref[h, :, lsl] = s2c_ref[pl.ds(h, N_KEYS, stride=PEER_HEADS), :]
        tsl = pl.ds(pl.multiple_of(tb * ROUTE_LANES, ROUTE_LANES), ROUTE_LANES)
        a = _top17_desc(s1_ref[:, tsl].reshape(N_KEYS, PEER_HEADS, ROUTE_LANES))
        b = _top17_desc(s2_ref[:, tsl].reshape(N_KEYS, PEER_HEADS, ROUTE_LANES))
        nt = PEER_TOPK + 1
        cands = [a[p] + b[q] for p in range(nt) for q in range(nt) if (p + 1) * (q + 1) <= nt]
        order = cands + [None] * (64 - len(cands))
        _bitonic_sort_desc(order)
        thr = 0.5 * (order[PEER_TOPK - 1] + order[PEER_TOPK])
        m = order[0]
        z = functools.reduce(
            jnp.add, [jnp.where(cv >= thr, jnp.exp(cv - m), 0.0) for cv in cands])
        st_ref[0, :, tsl] = thr
        st_ref[1, :, tsl] = a[0]
        st_ref[2, :, tsl] = b[0]
        st_ref[3, :, tsl] = 1.0 / z
        return carry

    lax.fori_loop(0, tm // ROUTE_LANES, chunk, 0)


def _peer_route(h2t, wq_t, kb1, kb2, tm=512):
    d, n = h2t.shape
    nk = N_KEYS * PEER_HEADS
    s_shape = jax.ShapeDtypeStruct((nk, n), F32)
    return pl.pallas_call(
        functools.partial(_route_kernel, tm=tm),
        grid=(n // tm,),
        in_specs=[pl.BlockSpec((d, tm), lambda i: (0, i)),
                  pl.BlockSpec((2 * nk, d), lambda i: (0, 0)),
                  pl.BlockSpec((nk, nk), lambda i: (0, 0)),
                  pl.BlockSpec((nk, nk), lambda i: (0, 0))],
        out_specs=[pl.BlockSpec((nk, tm), lambda i: (0, i)),
                   pl.BlockSpec((PEER_HEADS, N_KEYS, tm), lambda i: (0, 0, i)),
                   pl.BlockSpec((4, PEER_HEADS, tm), lambda i: (0, 0, i))],
        out_shape=[s_shape, jax.ShapeDtypeStruct((PEER_HEADS, N_KEYS, n), F32),
                   jax.ShapeDtypeStruct((4, PEER_HEADS, n), F32)],
        scratch_shapes=[pltpu.VMEM((nk, tm), F32),
                        pltpu.VMEM((nk, LANES), F32)],
        compiler_params=_cparams(("parallel",)),
        name="peer_route",
    )(h2t, wq_t, kb1, kb2)


def _peer_kernel(ht_ref, u_ref, vt_ref, s1_ref, s2h_ref, st_ref, acc_ref,
                 e2_ref, c1_ref, th_ref, z_ref, g_ref, w_ref, *, tm, te, ne, n_steps):
    s = pl.program_id(0)
    e1 = jnp.minimum(s, n_steps - 1) % ne
    prev = jnp.maximum(s - 1, 0)
    ni = te // N_KEYS
    slot = s % 2

    @pl.when(s == 0)
    def _():
        w_ref[...] = jnp.zeros(w_ref.shape, w_ref.dtype)

    @pl.when(prev % ne == 0)
    def _():
        acc_ref[...] = jnp.zeros(acc_ref.shape, F32)

    @pl.when(e1 == 0)
    def _():
        for h in range(PEER_HEADS):
            e2_ref[h] = jnp.exp(s2h_ref[h] - st_ref[2, h:h + 1, :])
        c1_ref[...] = jnp.exp(s1_ref[...] - st_ref[1][None]) * st_ref[3][None]
        th_ref[...] = st_ref[0][None] - s1_ref[...]

    z_ref[...] = jnp.dot(u_ref[...], ht_ref[...], preferred_element_type=F32)
    acc_ref[...] += jnp.dot(vt_ref[...], w_ref[1 - slot], preferred_element_type=F32)

    for i in range(ni):
        key1 = e1 * ni + i
        for tb in range(tm // LANES):
            tsl = slice(tb * LANES, (tb + 1) * LANES)
            for jb in range(N_KEYS // PEER_GATE_ROWS):
                jr = slice(jb * PEER_GATE_ROWS, (jb + 1) * PEER_GATE_ROWS)
                gate = None
                for h in range(PEER_HEADS):
                    hit = s2h_ref[h, jr, tsl] >= th_ref[key1, h:h + 1, tsl]
                    term = jnp.where(hit, e2_ref[h, jr, tsl] * c1_ref[key1, h:h + 1, tsl], 0.0)
                    gate = term if gate is None else gate + term
                g_ref[i * N_KEYS + jb * PEER_GATE_ROWS:i * N_KEYS + (jb + 1) * PEER_GATE_ROWS, tsl] = gate

    w_ref[slot] =(_gelu_sigmoid_form(z_ref[...]) * g_ref[...]).astype(w_ref.dtype)


def _peer_dense(h2t, u, vt, s1, s2h, st, tm=512, te=1024):
    d, n = h2t.shape
    ne = u.shape[0] // te
    n_steps = (n // tm) * ne

    def cur(s):
        return jnp.minimum(s, n_steps - 1)

    def prev(s):
        return jnp.maximum(s - 1, 0)

    return pl.pallas_call(
        functools.partial(_peer_kernel, tm=tm, te=te, ne=ne, n_steps=n_steps),
        grid=(n_steps + 1,),
        in_specs=[pl.BlockSpec((d, tm), lambda s: (0, cur(s) // ne)),
                  pl.BlockSpec((te, d), lambda s: (cur(s) % ne, 0)),
                  pl.BlockSpec((d, te), lambda s: (0, prev(s) % ne)),
                  pl.BlockSpec((N_KEYS, PEER_HEADS, tm), lambda s: (0, 0, cur(s) // ne)),
                  pl.BlockSpec((PEER_HEADS, N_KEYS, tm), lambda s: (0, 0, cur(s) // ne)),
                  pl.BlockSpec((4, PEER_HEADS, tm), lambda s: (0, 0, cur(s) // ne))],
        out_specs=pl.BlockSpec((d, tm), lambda s: (0, prev(s) // ne)),
        out_shape=jax.ShapeDtypeStruct((d, n), F32),
        scratch_shapes=[pltpu.VMEM((PEER_HEADS, N_KEYS, tm), F32),
                        pltpu.VMEM((N_KEYS, PEER_HEADS, tm), F32),
                        pltpu.VMEM((N_KEYS, PEER_HEADS, tm), F32),
                        pltpu.VMEM((te, tm), F32),
                        pltpu.VMEM((te, tm), F32),
                        pltpu.VMEM((2, te, tm), BF16)],
        compiler_params=_cparams(("arbitrary",)),
        name="peer_dense",
    )(h2t, u, vt, s1, s2h, st)


def _final_kernel(x1_ref, pt_ref, o_ref):
    o_ref[...] = x1_ref[...] + pt_ref[...].T


def _final_add(x1, peer_t, tm=512):
    n, d = x1.shape
    return pl.pallas_call(
        _final_kernel,
        grid=(n // tm,),
        in_specs=[pl.BlockSpec((tm, d), lambda i: (i, 0)),
                  pl.BlockSpec((d, tm), lambda i: (0, i))],
        out_specs=pl.BlockSpec((tm, d), lambda i: (i, 0)),
        out_shape=jax.ShapeDtypeStruct((n, d), F32),
        compiler_params=_cparams(("parallel",)),
        name="final_residual",
    )(x1, peer_t)


def _block_diag(w):
    g, a, b = w.shape
    eye = jnp.eye(g, dtype=w.dtype)
    return jnp.einsum("gab,gh->gahb", w, eye).reshape(g * a, g * b)


def _key_matrix(keys_c):
    h, k, dh = keys_c.shape
    eye = jnp.eye(h, dtype=keys_c.dtype)
    return jnp.einsum("hkd,hg->khgd", keys_c, eye).reshape(k * h, h * dh)


def kernel(x, mem, norm1_g, w_in, attn_q_norm_g, attn_k_norm_g, rel_bias, conv_w, conv_b, lru_wa, lru_ba, lru_wx, lru_bx, lru_lambda, mem_norm_g, w_mem_kv, xq_norm_g, xk_norm_g, w_gate, b_gate, w_branch, w_o, norm2_g, w_peer_q, peer_keys, peer_u, peer_v):
    batch, seq, d = x.shape
    n = batch * seq
    aw = A_HEADS * HEAD_DIM
    cw = C_HEADS * HEAD_DIM
    lw = LRU_WIDTH
    x2 = x.reshape(n, d)

    w_in_b = w_in.astype(BF16)
    tn = 512
    lru_tile0, qc_tile0 = 3 * aw // tn, (3 * aw + 2 * lw) // tn
    gains_a = jnp.concatenate([jnp.tile(attn_q_norm_g * HEAD_DIM ** -0.5, A_HEADS),
                               jnp.tile(attn_k_norm_g, A_HEADS),
                               jnp.ones((aw,), F32), jnp.tile(xq_norm_g, C_HEADS)]).reshape(1, -1)
    wbd = jnp.concatenate([_block_diag(lru_wa), _block_diag(lru_wx)], axis=1).astype(BF16)
    bab = jnp.concatenate([lru_ba, lru_bx]).reshape(1, 2 * lw)
    table = _attn_bias_table(rel_bias)
    hd = PEER_HEADS * N_KEYS
    wq_t = w_peer_q.reshape(d, PEER_HEADS, 2, HEAD_DIM).transpose(2, 1, 3, 0).reshape(2 * hd, d).astype(BF16)
    kb1 = _key_matrix(peer_keys[:, 0]).astype(BF16)
    kb2 = _key_matrix(peer_keys[:, 1]).astype(BF16)
    u_b = peer_u.astype(BF16)
    vt_b = peer_v.T.astype(BF16)

    h = _rmsnorm(x2, norm1_g)
    proj_a = _proj(h, w_in_b, lambda j: jnp.where(j < lru_tile0, j, j + (qc_tile0 - lru_tile0)),
                   gains_a, (2 * aw, 3 * aw), BF16, "in_proj_qkv", tn=tn)
    xg = _proj(h, w_in_b, lambda j: j + lru_tile0, jnp.ones((1, 2 * lw), F32),
               (0, 2 * lw), F32, "in_proj_lru", tn=tn)
    o_a = _band_attention(proj_a, table, batch, seq, qcol=0, kcol=1, vcol=2)
    o_b = _lru_branch(xg, conv_w, conv_b, wbd, bab, lru_lambda, batch, seq)
    k_mem, v_mem = _mem_kv(mem, mem_norm_g, w_mem_kv.astype(BF16), xk_norm_g)
    merged = _merge(h, o_a, o_b, proj_a, (3 * aw) // cw, k_mem, v_mem, w_gate.astype(BF16),
                    b_gate.reshape(3, d), w_branch.astype(BF16), seq)
    x1, h2t = _out_proj(x2, merged, w_o.astype(BF16), norm2_g)

    s1, s2h, st = _peer_route(h2t, wq_t, kb1, kb2)
    peer_t = _peer_dense(h2t, u_b, vt_b, s1.reshape(N_KEYS, PEER_HEADS, n), s2h, st)
    out = _final_add(x1, peer_t)
    return out.reshape(batch, seq, d)
```

```python
import functools

import numpy as np
import jax
import jax.numpy as jnp
from jax import lax
from jax.experimental import pallas as pl
from jax.experimental.pallas import tpu as pltpu

F32 = jnp.float32
BF16 = jnp.bfloat16

EPS = 1e-6
NEG = -1e30
LANES = 128
SUBLANES = 8
VMEM_LIMIT = 56 * 1024 * 1024

CHUNK = 64
N_PREV_CHUNKS = 8
REL_CLIP = 128
A_HEADS = 8
HEAD_DIM = 128
C_HEADS = 4
LRU_WIDTH = 512
LRU_C = 8.0
CONV_WIDTH = 4
PEER_HEADS = 8
N_KEYS = 128
PEER_TOPK = 16

PEER_TE = 1024
ROUTE_LANES = 256
ATT_TQ = 256
ATT_TK = ATT_TQ + N_PREV_CHUNKS * CHUNK


def _cparams(sem):
    return pltpu.CompilerParams(dimension_semantics=sem, vmem_limit_bytes=VMEM_LIMIT)


def _gelu(x):
    return 0.5 * x * (1.0 + jnp.tanh(0.7978845608028654 * (x + 0.044715 * (x * x * x))))


_GELU_K1 = -2.0 * 0.7978845608028654 * 1.4426950408889634
_GELU_K2 = _GELU_K1 * 0.044715


def _gelu_sigmoid_form(x):
    return x / (1.0 + jnp.exp2(x * (_GELU_K1 + _GELU_K2 * (x * x))))


def _sigmoid(x):
    return 1.0 / (1.0 + jnp.exp(-x))


def _rmsnorm_kernel(x_ref, g_ref, o_ref):
    x = x_ref[...]
    ms = jnp.mean(x * x, axis=-1, keepdims=True)
    o_ref[...] = (x * lax.rsqrt(ms + EPS) * g_ref[...]).astype(o_ref.dtype)


def _rmsnorm(x, g, tr=512):
    n, d = x.shape
    return pl.pallas_call(
        _rmsnorm_kernel,
        grid=(n // tr,),
        in_specs=[pl.BlockSpec((tr, d), lambda i: (i, 0)),
                  pl.BlockSpec((1, d), lambda i: (0, 0))],
        out_specs=pl.BlockSpec((tr, d), lambda i: (i, 0)),
        out_shape=jax.ShapeDtypeStruct((n, d), BF16),
        compiler_params=_cparams(("parallel",)),
        name="rmsnorm1",
    )(x, g.reshape(1, d))


def _proj_kernel(h_ref, w_ref, g_ref, o_ref, *, plain_lo, plain_hi, tn):
    acc = jnp.dot(h_ref[...], w_ref[...], preferred_element_type=F32)
    j = pl.program_id(1)
    plain = (j >= plain_lo) & (j < plain_hi)

    @pl.when(jnp.logical_not(plain))
    def _():
        for c in range(tn // HEAD_DIM):
            sl = slice(c * HEAD_DIM, (c + 1) * HEAD_DIM)
            blk = acc[:, sl]
            ms = jnp.mean(blk * blk, axis=-1, keepdims=True)
            o_ref[:, sl] = (blk * lax.rsqrt(ms + EPS) * g_ref[:, sl]).astype(o_ref.dtype)

    @pl.when(plain)
    def _():
        o_ref[...] = acc.astype(o_ref.dtype)


def _proj(h, w, gains, plain_cols, out_dtype, name, tm=1024, tn=512):
    n, d = h.shape
    nc = w.shape[1]
    kern = functools.partial(_proj_kernel, plain_lo=plain_cols[0] // tn, plain_hi=plain_cols[1] // tn, tn=tn)
    return pl.pallas_call(
        kern,
        grid=(n // tm, nc // tn),
        in_specs=[pl.BlockSpec((tm, d), lambda i, j: (i, 0)),
                  pl.BlockSpec((d, tn), lambda i, j: (0, j)),
                  pl.BlockSpec((1, tn), lambda i, j: (0, j))],
        out_specs=pl.BlockSpec((tm, tn), lambda i, j: (i, j)),
        out_shape=jax.ShapeDtypeStruct((n, nc), out_dtype),
        compiler_params=_cparams(("parallel", "arbitrary")),
        name=name,
    )(h, w, gains)


def _attn_kernel(q_ref, k0_ref, k1_ref, k2_ref, v0_ref, v1_ref, v2_ref, t_ref, o_ref):
    nt = (((1,), (1,)), ((), ()))
    for h in range(A_HEADS):
        sl = slice(h * HEAD_DIM, (h + 1) * HEAD_DIM)
        q = q_ref[:, sl]
        s = jnp.concatenate(
            [lax.dot_general(q, kr[:, sl], nt, preferred_element_type=F32)
             for kr in (k0_ref, k1_ref, k2_ref)], axis=1)
        s = s + t_ref[0, h]
        m = jnp.max(s, axis=-1, keepdims=True)
        p = jnp.exp(s - m)
        l = jnp.sum(p, axis=-1, keepdims=True)
        pb = p.astype(BF16)
        o = jnp.dot(pb[:, 0:ATT_TQ], v0_ref[:, sl], preferred_element_type=F32)
        o += jnp.dot(pb[:, ATT_TQ:2 * ATT_TQ], v1_ref[:, sl], preferred_element_type=F32)
        o += jnp.dot(pb[:, 2 * ATT_TQ:], v2_ref[:, sl], preferred_element_type=F32)
        o_ref[:, sl] = (o / l).astype(o_ref.dtype)


def _attn_bias_table(rel_bias):
    nh = rel_bias.shape[0]
    tq, tk = ATT_TQ, 3 * ATT_TQ
    qi = np.arange(tq)[:, None]
    kj = np.arange(tk)[None, :]
    qch = qi // CHUNK
    kch = kj // CHUNK - (2 * tq) // CHUNK
    band = (kch <= qch) & (kch >= qch - N_PREV_CHUNKS)
    n_far = 3 * tq - 1 - REL_CLIP + 1
    n_near = tq + tk - 1 - n_far - (2 * REL_CLIP - 1)
    g = jnp.concatenate([jnp.broadcast_to(rel_bias[:, 2 * REL_CLIP:], (nh, n_far)),
                         rel_bias[:, 1:2 * REL_CLIP][:, ::-1],
                         jnp.broadcast_to(rel_bias[:, :1], (nh, n_near))], axis=1).astype(F32)
    ln = tq + tk
    gp = jnp.pad(g, ((0, 0), (0, 1)))
    m = jnp.tile(gp, (1, tq))[:, :tq * (ln - 1)].reshape(nh, tq, ln - 1)
    bias = m[:, :, tq - 1:tq - 1 + tk]
    tabs = []
    for t in range(3):
        valid = band & (kj >= 2 * ATT_TQ - ATT_TQ * t)
        tabs.append(jnp.where(valid[None], bias, NEG))
    return jnp.stack(tabs)


def _band_attention(proj, table, batch, seq, qcol, kcol, vcol):
    n = proj.shape[0]
    aw = A_HEADS * HEAD_DIM
    nq = seq // ATT_TQ

    def qmap(b, t):
        return (b * nq + t, qcol)

    def kvmap(col, back):
        return lambda b, t: (b * nq + jnp.maximum(t - back, 0), col)

    blk = (ATT_TQ, aw)
    return pl.pallas_call(
        _attn_kernel,
        grid=(batch, nq),
        in_specs=[pl.BlockSpec(blk, qmap),
                  pl.BlockSpec(blk, kvmap(kcol, 2)), pl.BlockSpec(blk, kvmap(kcol, 1)),
                  pl.BlockSpec(blk, kvmap(kcol, 0)),
                  pl.BlockSpec(blk, kvmap(vcol, 2)), pl.BlockSpec(blk, kvmap(vcol, 1)),
                  pl.BlockSpec(blk, kvmap(vcol, 0)),
                  pl.BlockSpec((1, A_HEADS, ATT_TQ, 3 * ATT_TQ),
                               lambda b, t: (jnp.minimum(t, 2), 0, 0, 0))],
        out_specs=pl.BlockSpec(blk, lambda b, t: (b * nq + t, 0)),
        out_shape=jax.ShapeDtypeStruct((n, aw), BF16),
        compiler_params=_cparams(("parallel", "arbitrary")),
        name="band_attention",
    )(proj, proj, proj, proj, proj, proj, proj, table)


def _lru_kernel(xg_ref, cw_ref, cb_ref, wbd_ref, bab_ref, lam_ref, o_ref,
                xpad_ref, hprev_ref, hbuf_ref, *, tt):
    w = LRU_WIDTH
    t = pl.program_id(1)

    @pl.when(t == 0)
    def _():
        xpad_ref[0:SUBLANES, :] = jnp.zeros((SUBLANES, w), F32)
        hprev_ref[...] = jnp.zeros((1, w), F32)

    xl = xg_ref[:, 0:w]
    gl = xg_ref[:, w:2 * w]
    xpad_ref[SUBLANES:SUBLANES + tt, :] = xl
    xc = cw_ref[3:4, :] * xl + cb_ref[...]
    for k in range(CONV_WIDTH - 1):
        off = SUBLANES - (CONV_WIDTH - 1) + k
        xc = xc + cw_ref[k:k + 1, :] * xpad_ref[off:off + tt, :]
    xpad_ref[0:SUBLANES, :] = xl[tt - SUBLANES:tt, :]

    z = jnp.dot(xc.astype(BF16), wbd_ref[...], preferred_element_type=F32) + bab_ref[...]
    r = _sigmoid(z[:, 0:w])
    i = _sigmoid(z[:, w:2 * w])
    nl = -lam_ref[...]
    softplus = jnp.maximum(nl, 0.0) + jnp.log(1.0 + jnp.exp(-jnp.abs(nl)))
    log_a = -LRU_C * r * softplus
    a = jnp.exp(log_a)
    y = jnp.clip(1.0 - a * a, 1e-12, 1.0)
    u = (y * lax.rsqrt(y)) * (i * xc)

    sub = lax.broadcasted_iota(jnp.int32, (tt, w), 0) % SUBLANES
    d = 1
    while d < SUBLANES:
        keep = sub >= d
        a_sh = jnp.where(keep, pltpu.roll(a, d, axis=0), 1.0)
        u_sh = jnp.where(keep, pltpu.roll(u, d, axis=0), 0.0)
        u = a * u_sh + u
        a = a * a_sh
        d *= 2
    carry = hprev_ref[...]
    for g in range(tt // SUBLANES):
        rows = slice(g * SUBLANES, (g + 1) * SUBLANES)
        h = u[rows] + a[rows] * carry
        carry = h[SUBLANES - 1:SUBLANES, :]
        hbuf_ref[rows, :] = h
    hprev_ref[...] = carry
    o_ref[...] = (hbuf_ref[...] * _gelu(gl)).astype(o_ref.dtype)


def _lru_branch(xg, conv_w, conv_b, wbd, bab, lam, batch, seq, tt=512):
    n = xg.shape[0]
    w = LRU_WIDTH
    nt = seq // tt
    const = lambda b, t: (0, 0)
    return pl.pallas_call(
        functools.partial(_lru_kernel, tt=tt),
        grid=(batch, nt),
        in_specs=[pl.BlockSpec((tt, 2 * w), lambda b, t: (b * nt + t, 0)),
                  pl.BlockSpec((CONV_WIDTH, w), const),
                  pl.BlockSpec((1, w), const),
                  pl.BlockSpec((w, 2 * w), const),
                  pl.BlockSpec((1, 2 * w), const),
                  pl.BlockSpec((1, w), const)],
        out_specs=pl.BlockSpec((tt, w), lambda b, t: (b * nt + t, 0)),
        out_shape=jax.ShapeDtypeStruct((n, w), BF16),
        scratch_shapes=[pltpu.VMEM((tt + SUBLANES, w), F32), pltpu.VMEM((1, w), F32),
                        pltpu.VMEM((tt, w), F32)],
        compiler_params=_cparams(("parallel", "arbitrary")),
        name="rg_lru",
    )(xg, conv_w, conv_b.reshape(1, w), wbd, bab, lam.reshape(1, w))


def _memkv_kernel(mem_ref, g_ref, w_ref, kg_ref, k_ref, v_ref):
    x = mem_ref[0]
    ms = jnp.mean(x * x, axis=-1, keepdims=True)
    mn = (x * lax.rsqrt(ms + EPS) * g_ref[...]).astype(BF16)
    kv = jnp.dot(mn, w_ref[...], preferred_element_type=F32)
    cw = C_HEADS * HEAD_DIM
    for h in range(C_HEADS):
        sl = slice(h * HEAD_DIM, (h + 1) * HEAD_DIM)
        blk = kv[:, sl]
        ms = jnp.mean(blk * blk, axis=-1, keepdims=True)
        k_ref[0, :, sl] = (blk * lax.rsqrt(ms + EPS) * kg_ref[...]).astype(k_ref.dtype)
    v_ref[0] = kv[:, cw:2 * cw].astype(v_ref.dtype)


def _mem_kv(mem, g, w_kv, kg):
    b, m, d = mem.shape
    cw = C_HEADS * HEAD_DIM
    out = jax.ShapeDtypeStruct((b, m, cw), BF16)
    return pl.pallas_call(
        _memkv_kernel,
        grid=(b,),
        in_specs=[pl.BlockSpec((1, m, d), lambda i: (i, 0, 0)),
                  pl.BlockSpec((1, d), lambda i: (0, 0)),
                  pl.BlockSpec((d, 2 * cw), lambda i: (0, 0)),
                  pl.BlockSpec((1, HEAD_DIM), lambda i: (0, 0))],
        out_specs=[pl.BlockSpec((1, m, cw), lambda i: (i, 0, 0)),
                   pl.BlockSpec((1, m, cw), lambda i: (i, 0, 0))],
        out_shape=[out, out],
        compiler_params=_cparams(("parallel",)),
        name="mem_kv",
    )(mem, g.reshape(1, d), w_kv, kg.reshape(1, HEAD_DIM))


def _merge_kernel(h_ref, oa_ref, ob_ref, qc_ref, km_ref, vm_ref, wg0_ref, wg1_ref, wg2_ref, bg_ref,
                  wba_ref, wbb_ref, wbc_ref, o_ref, oc_ref):
    j = pl.program_id(1)

    @pl.when(j == 0)
    def _():
        scale = HEAD_DIM ** -0.5
        nt = (((1,), (1,)), ((), ()))
        for hh in range(C_HEADS):
            sl = slice(hh * HEAD_DIM, (hh + 1) * HEAD_DIM)
            s = lax.dot_general(qc_ref[:, sl], km_ref[0, :, sl], nt, preferred_element_type=F32) * scale
            m = jnp.max(s, axis=-1, keepdims=True)
            p = jnp.exp(s - m)
            l = jnp.sum(p, axis=-1, keepdims=True)
            o = jnp.dot(p.astype(BF16), vm_ref[0, :, sl], preferred_element_type=F32)
            oc_ref[:, sl] = (o / l).astype(oc_ref.dtype)

    h = h_ref[...]
    merged = None
    for br, (o_br, wg_ref, wb_ref) in enumerate(((oa_ref[...], wg0_ref, wba_ref),
                                                 (ob_ref[...], wg1_ref, wbb_ref),
                                                 (oc_ref[...], wg2_ref, wbc_ref))):
        y = jnp.dot(o_br, wb_ref[...], preferred_element_type=F32)
        g = _sigmoid(jnp.dot(h, wg_ref[...], preferred_element_type=F32) + bg_ref[br:br + 1, :])
        merged = g * y if merged is None else merged + g * y
    o_ref[...] = merged.astype(o_ref.dtype)


def _merge(h, o_a, o_b, proj_a, qc_col, k_mem, v_mem, w_gate, b_gate3, w_branch, seq, tm=1024, tn=512):
    n, d = h.shape
    aw = A_HEADS * HEAD_DIM
    cw = C_HEADS * HEAD_DIM
    n_mem = k_mem.shape[1]
    ncol = d // tn
    tiles_per_seq = seq // tm
    return pl.pallas_call(
        _merge_kernel,
        grid=(n // tm, ncol),
        in_specs=[pl.BlockSpec((tm, d), lambda i, j: (i, 0)),
                  pl.BlockSpec((tm, aw), lambda i, j: (i, 0)),
                  pl.BlockSpec((tm, LRU_WIDTH), lambda i, j: (i, 0)),
                  pl.BlockSpec((tm, cw), lambda i, j: (i, qc_col)),
                  pl.BlockSpec((1, n_mem, cw), lambda i, j: (i // tiles_per_seq, 0, 0)),
                  pl.BlockSpec((1, n_mem, cw), lambda i, j: (i // tiles_per_seq, 0, 0)),
                  pl.BlockSpec((d, tn), lambda i, j: (0, j)),
                  pl.BlockSpec((d, tn), lambda i, j: (0, j + ncol)),
                  pl.BlockSpec((d, tn), lambda i, j: (0, j + 2 * ncol)),
                  pl.BlockSpec((3, tn), lambda i, j: (0, j)),
                  pl.BlockSpec((aw, tn), lambda i, j: (0, j)),
                  pl.BlockSpec((LRU_WIDTH, tn), lambda i, j: (aw // LRU_WIDTH, j)),
                  pl.BlockSpec((cw, tn), lambda i, j: ((aw + LRU_WIDTH) // cw, j))],
        out_specs=pl.BlockSpec((tm, tn), lambda i, j: (i, j)),
        out_shape=jax.ShapeDtypeStruct((n, d), BF16),
        scratch_shapes=[pltpu.VMEM((tm, cw), BF16)],
        compiler_params=_cparams(("parallel", "arbitrary")),
        name="gated_merge",
    )(h, o_a, o_b, proj_a, k_mem, v_mem, w_gate, w_gate, w_gate, b_gate3, w_branch, w_branch, w_branch)


def _oproj_kernel(x_ref, m_ref, w_ref, g_ref, x1_ref, h2t_ref):
    x1 = x_ref[...] + jnp.dot(m_ref[...], w_ref[...], preferred_element_type=F32)
    x1_ref[...] = x1
    ms = jnp.mean(x1 * x1, axis=-1, keepdims=True)
    h2 = x1 * lax.rsqrt(ms + EPS) * g_ref[...]
    h2t_ref[...] = h2.T.astype(h2t_ref.dtype)


def _out_proj(x, merged, w_o, g2, tm=512):
    n, d = x.shape
    return pl.pallas_call(
        _oproj_kernel,
        grid=(n // tm,),
        in_specs=[pl.BlockSpec((tm, d), lambda i: (i, 0)),
                  pl.BlockSpec((tm, d), lambda i: (i, 0)),
                  pl.BlockSpec((d, d), lambda i: (0, 0)),
                  pl.BlockSpec((1, d), lambda i: (0, 0))],
        out_specs=[pl.BlockSpec((tm, d), lambda i: (i, 0)),
                   pl.BlockSpec((d, tm), lambda i: (0, i))],
        out_shape=[jax.ShapeDtypeStruct((n, d), F32), jax.ShapeDtypeStruct((d, n), BF16)],
        compiler_params=_cparams(("parallel",)),
        name="out_proj_norm2",
    )(x, merged, w_o, g2.reshape(1, d))


def _cmpx(v, i, j):
    a, b = v[i], v[j]
    if b is None:
        return
    if a is None:
        v[i], v[j] = b, None
        return
    v[i], v[j] = jnp.maximum(a, b), jnp.minimum(a, b)


def _bitonic_merge_desc(v, lo, n):
    j = n // 2
    while j >= 1:
        for i in range(lo, lo + n):
            if (i - lo) & j == 0:
                _cmpx(v, i, i + j)
        j //= 2


def _bitonic_sort_desc(v):
    n = len(v)
    k = 2
    while k <= n:
        j = k // 2
        while j >= 1:
            for i in range(n):
                l = i ^ j
                if l > i:
                    if i & k == 0 or k == n:
                        _cmpx(v, i, l)
                    else:
                        _cmpx(v, l, i)
            j //= 2
        k *= 2


def _top17_desc(x):
    k = PEER_TOPK
    groups = []
    for g in range(x.shape[0] // k):
        run = [x[g * k + r] for r in range(k)]
        _bitonic_sort_desc(run)
        groups.append(run)
    dropped = None
    while len(groups) > 1:
        merged = []
        for g in range(0, len(groups), 2):
            p, q = groups[g], groups[g + 1]
            hi = [jnp.maximum(p[r], q[k - 1 - r]) for r in range(k)]
            lo = functools.reduce(jnp.maximum, [jnp.minimum(p[r], q[k - 1 - r]) for r in range(k)])
            dropped = lo if dropped is None else jnp.maximum(dropped, lo)
            _bitonic_merge_desc(hi, 0, k)
            merged.append(hi)
        groups = merged
    return groups[0] + [dropped]


def _route_kernel(ht_ref, wq_ref, kb1_ref, kb2_ref, s1_ref, s2h_ref, st_ref,
                  s2_ref, s2c_ref, *, tm):
    nk = N_KEYS * PEER_HEADS
    qt = jnp.dot(wq_ref[...], ht_ref[...], preferred_element_type=F32)
    s1_ref[...] = jnp.dot(kb1_ref[...], qt[0:nk].astype(BF16), preferred_element_type=F32)
    s2_ref[...] = jnp.dot(kb2_ref[...], qt[nk:2 * nk].astype(BF16), preferred_element_type=F32)

    def chunk(tb, carry):
        for sub in range(ROUTE_LANES // LANES):
            lsl = pl.ds(pl.multiple_of(tb * ROUTE_LANES + sub * LANES, LANES), LANES)
            s2c_ref[...] = s2_ref[:, lsl]
            for h in range(PEER_HEADS):
                s2h_ref[h, :, lsl] = s2c_ref[pl.ds(h, N_KEYS, stride=PEER_HEADS), :]
        tsl = pl.ds(pl.multiple_of(tb * ROUTE_LANES, ROUTE_LANES), ROUTE_LANES)
        a = _top17_desc(s1_ref[:, tsl].reshape(N_KEYS, PEER_HEADS, ROUTE_LANES))
        b = _top17_desc(s2_ref[:, tsl].reshape(N_KEYS, PEER_HEADS, ROUTE_LANES))
        nt = PEER_TOPK + 1
        cands = [a[p] + b[q] for p in range(nt) for q in range(nt) if (p + 1) * (q + 1) <= nt]
        order = cands + [None] * (64 - len(cands))
        _bitonic_sort_desc(order)
        thr = 0.5 * (order[PEER_TOPK - 1] + order[PEER_TOPK])
        m = order[0]
        z = functools.reduce(
            jnp.add, [jnp.where(cv >= thr, jnp.exp(cv - m), 0.0) for cv in cands])
        st_ref[0, :, tsl] = thr
        st_ref[1, :, tsl] = a[0]
        st_ref[2, :, tsl] = b[0]
        st_ref[3, :, tsl] = 1.0 / z
        return carry

    lax.fori_loop(0, tm // ROUTE_LANES, chunk, 0)


def _peer_route(h2t, wq_t, kb1, kb2, tm=512):
    d, n = h2t.shape
    nk = N_KEYS * PEER_HEADS
    s_shape = jax.ShapeDtypeStruct((nk, n), F32)
    return pl.pallas_call(
        functools.partial(_route_kernel, tm=tm),
        grid=(n // tm,),
        in_specs=[pl.BlockSpec((d, tm), lambda i: (0, i)),
                  pl.BlockSpec((2 * nk, d), lambda i: (0, 0)),
                  pl.BlockSpec((nk, nk), lambda i: (0, 0)),
                  pl.BlockSpec((nk, nk), lambda i: (0, 0))],
        out_specs=[pl.BlockSpec((nk, tm), lambda i: (0, i)),
                   pl.BlockSpec((PEER_HEADS, N_KEYS, tm), lambda i: (0, 0, i)),
                   pl.BlockSpec((4, PEER_HEADS, tm), lambda i: (0, 0, i))],
        out_shape=[s_shape, jax.ShapeDtypeStruct((PEER_HEADS, N_KEYS, n), F32),
                   jax.ShapeDtypeStruct((4, PEER_HEADS, n), F32)],
        scratch_shapes=[pltpu.VMEM((nk, tm), F32),
                        pltpu.VMEM((nk, LANES), F32)],
        compiler_params=_cparams(("parallel",)),
        name="peer_route",
    )(h2t, wq_t, kb1, kb2)


def _peer_kernel(ht_ref, u_ref, vt_ref, s1_ref, s2h_ref, st_ref, acc_ref,
                 e2_ref, c1_ref, th_ref, z_ref, g_ref, w_ref, *, tm, te, ne, n_steps):
    s = pl.program_id(0)
    e1 = jnp.minimum(s, n_steps - 1) % ne
    prev = jnp.maximum(s - 1, 0)
    ni = te // N_KEYS
    slot = s % 2

    @pl.when(s == 0)
    def _():
        w_ref[...] = jnp.zeros(w_ref.shape, w_ref.dtype)

    @pl.when(prev % ne == 0)
    def _():
        acc_ref[...] = jnp.zeros(acc_ref.shape, F32)

    @pl.when(e1 == 0)
    def _():
        for h in range(PEER_HEADS):
            e2_ref[h] = jnp.exp(s2h_ref[h] - st_ref[2, h:h + 1, :])
        c1_ref[...] = jnp.exp(s1_ref[...] - st_ref[1][None]) * st_ref[3][None]
        th_ref[...] = st_ref[0][None] - s1_ref[...]

    z_ref[...] = jnp.dot(u_ref[...], ht_ref[...], preferred_element_type=F32)
    acc_ref[...] += jnp.dot(vt_ref[0], w_ref[1 - slot], preferred_element_type=F32)

    for i in range(ni):
        key1 = e1 * ni + i
        for tb in range(tm // LANES):
            tsl = slice(tb * LANES, (tb + 1) * LANES)
            gate = None
            for h in range(PEER_HEADS):
                hit = s2h_ref[h, :, tsl] >= th_ref[key1, h:h + 1, tsl]
                term = jnp.where(hit, e2_ref[h, :, tsl] * c1_ref[key1, h:h + 1, tsl], 0.0)
                gate = term if gate is None else gate + term
            g_ref[i * N_KEYS:(i + 1) * N_KEYS, tsl] = gate

    w_ref[slot] = (_gelu_sigmoid_form(z_ref[...]) * g_ref[...]).astype(w_ref.dtype)


def _peer_dense(h2t, u, vt, s1, s2h, st, tm=512, te=PEER_TE):
    d, n = h2t.shape
    ne = u.shape[0] // te
    n_steps = (n // tm) * ne

    def cur(s):
        return jnp.minimum(s, n_steps - 1)

    def prev(s):
        return jnp.maximum(s - 1, 0)

    return pl.pallas_call(
        functools.partial(_peer_kernel, tm=tm, te=te, ne=ne, n_steps=n_steps),
        grid=(n_steps + 1,),
        in_specs=[pl.BlockSpec((d, tm), lambda s: (0, cur(s) // ne)),
                  pl.BlockSpec((te, d), lambda s: (cur(s) % ne, 0)),
                  pl.BlockSpec((1, d, te), lambda s: (prev(s) % ne, 0, 0)),
                  pl.BlockSpec((N_KEYS, PEER_HEADS, tm), lambda s: (0, 0, cur(s) // ne)),
                  pl.BlockSpec((PEER_HEADS, N_KEYS, tm), lambda s: (0, 0, cur(s) // ne)),
                  pl.BlockSpec((4, PEER_HEADS, tm), lambda s: (0, 0, cur(s) // ne))],
        out_specs=pl.BlockSpec((d, tm), lambda s: (0, prev(s) // ne)),
        out_shape=jax.ShapeDtypeStruct((d, n), F32),
        scratch_shapes=[pltpu.VMEM((PEER_HEADS, N_KEYS, tm), F32),
                        pltpu.VMEM((N_KEYS, PEER_HEADS, tm), F32),
                        pltpu.VMEM((N_KEYS, PEER_HEADS, tm), F32),
                        pltpu.VMEM((te, tm), F32),
                        pltpu.VMEM((te, tm), F32),
                        pltpu.VMEM((2, te, tm), BF16)],
        compiler_params=_cparams(("arbitrary",)),
        name="peer_dense",
    )(h2t, u, vt, s1, s2h, st)


def _final_kernel(x1_ref, pt_ref, o_ref):
    o_ref[...] = x1_ref[...] + pt_ref[...].T


def _final_add(x1, peer_t, tm=512):
    n, d = x1.shape
    return pl.pallas_call(
        _final_kernel,
        grid=(n // tm,),
        in_specs=[pl.BlockSpec((tm, d), lambda i: (i, 0)),
                  pl.BlockSpec((d, tm), lambda i: (0, i))],
        out_specs=pl.BlockSpec((tm, d), lambda i: (i, 0)),
        out_shape=jax.ShapeDtypeStruct((n, d), F32),
        compiler_params=_cparams(("parallel",)),
        name="final_residual",
    )(x1, peer_t)


def _block_diag(w):
    g, a, b = w.shape
    eye = jnp.eye(g, dtype=w.dtype)
    return jnp.einsum("gab,gh->gahb", w, eye).reshape(g * a, g * b)


def _key_matrix(keys_c):
    h, k, dh = keys_c.shape
    eye = jnp.eye(h, dtype=keys_c.dtype)
    return jnp.einsum("hkd,hg->khgd", keys_c, eye).reshape(k * h, h * dh)


def kernel(x, mem, norm1_g, w_in, attn_q_norm_g, attn_k_norm_g, rel_bias, conv_w, conv_b, lru_wa, lru_ba, lru_wx, lru_bx, lru_lambda, mem_norm_g, w_mem_kv, xq_norm_g, xk_norm_g, w_gate, b_gate, w_branch, w_o, norm2_g, w_peer_q, peer_keys, peer_u, peer_v):
    batch, seq, d = x.shape
    n = batch * seq
    aw = A_HEADS * HEAD_DIM
    cw = C_HEADS * HEAD_DIM
    lw = LRU_WIDTH
    x2 = x.reshape(n, d)

    cuts = np.cumsum([aw, aw, aw, lw, lw])
    wq, wk, wv, wxl, wgl, wqc = jnp.split(w_in, cuts, axis=1)
    w_a = jnp.concatenate([wq, wk, wv, wqc], axis=1).astype(BF16)
    w_b = jnp.concatenate([wxl, wgl], axis=1).astype(BF16)
    gains_a = jnp.concatenate([jnp.tile(attn_q_norm_g * HEAD_DIM ** -0.5, A_HEADS),
                               jnp.tile(attn_k_norm_g, A_HEADS),
                               jnp.ones((aw,), F32), jnp.tile(xq_norm_g, C_HEADS)]).reshape(1, -1)
    wbd = jnp.concatenate([_block_diag(lru_wa), _block_diag(lru_wx)], axis=1).astype(BF16)
    bab = jnp.concatenate([lru_ba, lru_bx]).reshape(1, 2 * lw)
    table = _attn_bias_table(rel_bias)
    hd = PEER_HEADS * N_KEYS
    wq_t = w_peer_q.reshape(d, PEER_HEADS, 2, HEAD_DIM).transpose(2, 1, 3, 0).reshape(2 * hd, d).astype(BF16)
    kb1 = _key_matrix(peer_keys[:, 0]).astype(BF16)
    kb2 = _key_matrix(peer_keys[:, 1]).astype(BF16)
    u_b = peer_u.astype(BF16)
    vt_b = peer_v.reshape(-1, PEER_TE, d).transpose(0, 2, 1).astype(BF16)

    h = _rmsnorm(x2, norm1_g)
    proj_a = _proj(h, w_a, gains_a, (2 * aw, 3 * aw), BF16, "in_proj_qkv")
    xg = _proj(h, w_b, jnp.ones((1, 2 * lw), F32), (0, 2 * lw), F32, "in_proj_lru")
    o_a = _band_attention(proj_a, table, batch, seq, qcol=0, kcol=1, vcol=2)
    o_b = _lru_branch(xg, conv_w, conv_b, wbd, bab, lru_lambda, batch, seq)
    k_mem, v_mem = _mem_kv(mem, mem_norm_g, w_mem_kv.astype(BF16), xk_norm_g)
    merged = _merge(h, o_a, o_b, proj_a, (3 * aw) // cw, k_mem, v_mem, w_gate.astype(BF16),
                    b_gate.reshape(3, d), w_branch.astype(BF16), seq)
    x1, h2t = _out_proj(x2, merged, w_o.astype(BF16), norm2_g)

    s1, s2h, st = _peer_route(h2t, wq_t, kb1, kb2)
    peer_t = _peer_dense(h2t, u_b, vt_b, s1.reshape(N_KEYS, PEER_HEADS, n), s2h, st)
    out = _final_add(x1, peer_t)
    return out.reshape(batch, seq, d)
```

```python
import functools

import numpy as np
import jax
import jax.numpy as jnp
from jax import lax
from jax.experimental import pallas as pl
from jax.experimental.pallas import tpu as pltpu

F32 = jnp.float32
BF16 = jnp.bfloat16

EPS = 1e-6
NEG = -1e30
LANES = 128
SUBLANES = 8
VMEM_LIMIT = 56 * 1024 * 1024

CHUNK = 64
N_PREV_CHUNKS = 8
REL_CLIP = 128
A_HEADS = 8
HEAD_DIM = 128
C_HEADS = 4
LRU_WIDTH = 512
LRU_C = 8.0
CONV_WIDTH = 4
PEER_HEADS = 8
N_KEYS = 128
PEER_TOPK = 16

PEER_TE = 1024
ROUTE_LANES = 256
ATT_TQ = 256
ATT_TK = ATT_TQ + N_PREV_CHUNKS * CHUNK


def _cparams(sem):
    return pltpu.CompilerParams(dimension_semantics=sem, vmem_limit_bytes=VMEM_LIMIT)


def _gelu(x):
    return 0.5 * x * (1.0 + jnp.tanh(0.7978845608028654 * (x + 0.044715 * (x * x * x))))


_GELU_K1 = -2.0 * 0.7978845608028654 * 1.4426950408889634
_GELU_K2 = _GELU_K1 * 0.044715


def _gelu_sigmoid_form(x):
    return x / (1.0 + jnp.exp2(x * (_GELU_K1 + _GELU_K2 * (x * x))))


def _sigmoid(x):
    return 1.0 / (1.0 + jnp.exp(-x))


def _rmsnorm_kernel(x_ref, g_ref, o_ref):
    x = x_ref[...]
    ms = jnp.mean(x * x, axis=-1, keepdims=True)
    o_ref[...] = (x * lax.rsqrt(ms + EPS) * g_ref[...]).astype(o_ref.dtype)


def _rmsnorm(x, g, tr=512):
    n, d = x.shape
    return pl.pallas_call(
        _rmsnorm_kernel,
        grid=(n // tr,),
        in_specs=[pl.BlockSpec((tr, d), lambda i: (i, 0)),
                  pl.BlockSpec((1, d), lambda i: (0, 0))],
        out_specs=pl.BlockSpec((tr, d), lambda i: (i, 0)),
        out_shape=jax.ShapeDtypeStruct((n, d), BF16),
        compiler_params=_cparams(("parallel",)),
        name="rmsnorm1",
    )(x, g.reshape(1, d))


def _proj_kernel(h_ref, w_ref, g_ref, o_ref, *, plain_lo, plain_hi, tn):
    acc = jnp.dot(h_ref[...], w_ref[...], preferred_element_type=F32)
    j = pl.program_id(1)
    plain = (j >= plain_lo) & (j < plain_hi)

    @pl.when(jnp.logical_not(plain))
    def _():
        for c in range(tn // HEAD_DIM):
            sl = slice(c * HEAD_DIM, (c + 1) * HEAD_DIM)
            blk = acc[:, sl]
            ms = jnp.mean(blk * blk, axis=-1, keepdims=True)
            o_ref[:, sl] = (blk * lax.rsqrt(ms + EPS) * g_ref[:, sl]).astype(o_ref.dtype)

    @pl.when(plain)
    def _():
        o_ref[...] = acc.astype(o_ref.dtype)


def _proj(h, w, gains, plain_cols, out_dtype, name, tm=2048, tn=512):
    n, d = h.shape
    nc = w.shape[1]
    kern = functools.partial(_proj_kernel, plain_lo=plain_cols[0] // tn, plain_hi=plain_cols[1] // tn, tn=tn)
    return pl.pallas_call(
        kern,
        grid=(n // tm, nc // tn),
        in_specs=[pl.BlockSpec((tm, d), lambda i, j: (i, 0)),
                  pl.BlockSpec((d, tn), lambda i, j: (0, j)),
                  pl.BlockSpec((1, tn), lambda i, j: (0, j))],
        out_specs=pl.BlockSpec((tm, tn), lambda i, j: (i, j)),
        out_shape=jax.ShapeDtypeStruct((n, nc), out_dtype),
        compiler_params=_cparams(("parallel", "arbitrary")),
        name=name,
    )(h, w, gains)


def _attn_kernel(q_ref, k0_ref, k1_ref, k2_ref, v0_ref, v1_ref, v2_ref, t_ref, o_ref):
    nt = (((1,), (1,)), ((), ()))
    for h in range(A_HEADS):
        sl = slice(h * HEAD_DIM, (h + 1) * HEAD_DIM)
        q = q_ref[:, sl]
        s = jnp.concatenate(
            [lax.dot_general(q, kr[:, sl], nt, preferred_element_type=F32)
             for kr in (k0_ref, k1_ref, k2_ref)], axis=1)
        s = s + t_ref[0, h]
        m = jnp.max(s, axis=-1, keepdims=True)
        p = jnp.exp(s - m)
        l = jnp.sum(p, axis=-1, keepdims=True)
        pb = p.astype(BF16)
        o = jnp.dot(pb[:, 0:ATT_TQ], v0_ref[:, sl], preferred_element_type=F32)
        o += jnp.dot(pb[:, ATT_TQ:2 * ATT_TQ], v1_ref[:, sl], preferred_element_type=F32)
        o += jnp.dot(pb[:, 2 * ATT_TQ:], v2_ref[:, sl], preferred_element_type=F32)
        o_ref[:, sl] = (o / l).astype(o_ref.dtype)


def _attn_bias_table(rel_bias):
    nh = rel_bias.shape[0]
    tq, tk = ATT_TQ, 3 * ATT_TQ
    qi = np.arange(tq)[:, None]
    kj = np.arange(tk)[None, :]
    qch = qi // CHUNK
    kch = kj // CHUNK - (2 * tq) // CHUNK
    band = (kch <= qch) & (kch >= qch - N_PREV_CHUNKS)
    n_far = 3 * tq - 1 - REL_CLIP + 1
    n_near = tq + tk - 1 - n_far - (2 * REL_CLIP - 1)
    g = jnp.concatenate([jnp.broadcast_to(rel_bias[:, 2 * REL_CLIP:], (nh, n_far)),
                         rel_bias[:, 1:2 * REL_CLIP][:, ::-1],
                         jnp.broadcast_to(rel_bias[:, :1], (nh, n_near))], axis=1).astype(F32)
    ln = tq + tk
    gp = jnp.pad(g, ((0, 0), (0, 1)))
    m = jnp.tile(gp, (1, tq))[:, :tq * (ln - 1)].reshape(nh, tq, ln - 1)
    bias = m[:, :, tq - 1:tq - 1 + tk]
    tabs = []
    for t in range(3):
        valid = band & (kj >= 2 * ATT_TQ - ATT_TQ * t)
        tabs.append(jnp.where(valid[None], bias, NEG))
    return jnp.stack(tabs)


def _band_attention(proj, table, batch, seq, qcol, kcol, vcol):
    n = proj.shape[0]
    aw = A_HEADS * HEAD_DIM
    nq = seq // ATT_TQ

    def qmap(b, t):
        return (b * nq + t, qcol)

    def kvmap(col, back):
        return lambda b, t: (b * nq + jnp.maximum(t - back, 0), col)

    blk = (ATT_TQ, aw)
    return pl.pallas_call(
        _attn_kernel,
        grid=(batch, nq),
        in_specs=[pl.BlockSpec(blk, qmap),
                  pl.BlockSpec(blk, kvmap(kcol, 2)), pl.BlockSpec(blk, kvmap(kcol, 1)),
                  pl.BlockSpec(blk, kvmap(kcol, 0)),
                  pl.BlockSpec(blk, kvmap(vcol, 2)), pl.BlockSpec(blk, kvmap(vcol, 1)),
                  pl.BlockSpec(blk, kvmap(vcol, 0)),
                  pl.BlockSpec((1, A_HEADS, ATT_TQ, 3 * ATT_TQ),
                               lambda b, t: (jnp.minimum(t, 2), 0, 0, 0))],
        out_specs=pl.BlockSpec(blk, lambda b, t: (b * nq + t, 0)),
        out_shape=jax.ShapeDtypeStruct((n, aw), BF16),
        compiler_params=_cparams(("parallel", "arbitrary")),
        name="band_attention",
    )(proj, proj, proj, proj, proj, proj, proj, table)


def _lru_kernel(xg_ref, cw_ref, cb_ref, wbd_ref, bab_ref, lam_ref, o_ref,
                xpad_ref, hprev_ref, hbuf_ref, *, tt):
    w = LRU_WIDTH
    t = pl.program_id(1)

    @pl.when(t == 0)
    def _():
        xpad_ref[0:SUBLANES, :] = jnp.zeros((SUBLANES, w), F32)
        hprev_ref[...] = jnp.zeros((1, w), F32)

    xl = xg_ref[:, 0:w]
    gl = xg_ref[:, w:2 * w]
    xpad_ref[SUBLANES:SUBLANES + tt, :] = xl
    xc = cw_ref[3:4, :] * xl + cb_ref[...]
    for k in range(CONV_WIDTH - 1):
        off = SUBLANES - (CONV_WIDTH - 1) + k
        xc = xc + cw_ref[k:k + 1, :] * xpad_ref[off:off + tt, :]
    xpad_ref[0:SUBLANES, :] = xl[tt - SUBLANES:tt, :]

    z = jnp.dot(xc.astype(BF16), wbd_ref[...], preferred_element_type=F32) + bab_ref[...]
    r = _sigmoid(z[:, 0:w])
    i = _sigmoid(z[:, w:2 * w])
    nl = -lam_ref[...]
    softplus = jnp.maximum(nl, 0.0) + jnp.log(1.0 + jnp.exp(-jnp.abs(nl)))
    log_a = -LRU_C * r * softplus
    a = jnp.exp(log_a)
    y = jnp.clip(1.0 - a * a, 1e-12, 1.0)
    u = (y * lax.rsqrt(y)) * (i * xc)

    sub = lax.broadcasted_iota(jnp.int32, (tt, w), 0) % SUBLANES
    d = 1
    while d < SUBLANES:
        keep = sub >= d
        a_sh = jnp.where(keep, pltpu.roll(a, d, axis=0), 1.0)
        u_sh = jnp.where(keep, pltpu.roll(u, d, axis=0), 0.0)
        u = a * u_sh + u
        a = a * a_sh
        d *= 2
    carry = hprev_ref[...]
    for g in range(tt // SUBLANES):
        rows = slice(g * SUBLANES, (g + 1) * SUBLANES)
        h = u[rows] + a[rows] * carry
        carry = h[SUBLANES - 1:SUBLANES, :]
        hbuf_ref[rows, :] = h
    hprev_ref[...] = carry
    o_ref[...] = (hbuf_ref[...] * _gelu(gl)).astype(o_ref.dtype)


def _lru_branch(xg, conv_w, conv_b, wbd, bab, lam, batch, seq, tt=512):
    n = xg.shape[0]
    w = LRU_WIDTH
    nt = seq // tt
    const = lambda b, t: (0, 0)
    return pl.pallas_call(
        functools.partial(_lru_kernel, tt=tt),
        grid=(batch, nt),
        in_specs=[pl.BlockSpec((tt, 2 * w), lambda b, t: (b * nt + t, 0)),
                  pl.BlockSpec((CONV_WIDTH, w), const),
                  pl.BlockSpec((1, w), const),
                  pl.BlockSpec((w, 2 * w), const),
                  pl.BlockSpec((1, 2 * w), const),
                  pl.BlockSpec((1, w), const)],
        out_specs=pl.BlockSpec((tt, w), lambda b, t: (b * nt + t, 0)),
        out_shape=jax.ShapeDtypeStruct((n, w), BF16),
        scratch_shapes=[pltpu.VMEM((tt + SUBLANES, w), F32), pltpu.VMEM((1, w), F32),
                        pltpu.VMEM((tt, w), F32)],
        compiler_params=_cparams(("parallel", "arbitrary")),
        name="rg_lru",
    )(xg, conv_w, conv_b.reshape(1, w), wbd, bab, lam.reshape(1, w))


def _memkv_kernel(mem_ref, g_ref, w_ref, kg_ref, k_ref, v_ref):
    x = mem_ref[0]
    ms = jnp.mean(x * x, axis=-1, keepdims=True)
    mn = (x * lax.rsqrt(ms + EPS) * g_ref[...]).astype(BF16)
    kv = jnp.dot(mn, w_ref[...], preferred_element_type=F32)
    cw = C_HEADS * HEAD_DIM
    for h in range(C_HEADS):
        sl = slice(h * HEAD_DIM, (h + 1) * HEAD_DIM)
        blk = kv[:, sl]
        ms = jnp.mean(blk * blk, axis=-1, keepdims=True)
        k_ref[0, :, sl] = (blk * lax.rsqrt(ms + EPS) * kg_ref[...]).astype(k_ref.dtype)
    v_ref[0] = kv[:, cw:2 * cw].astype(v_ref.dtype)


def _mem_kv(mem, g, w_kv, kg):
    b, m, d = mem.shape
    cw = C_HEADS * HEAD_DIM
    out = jax.ShapeDtypeStruct((b, m, cw), BF16)
    return pl.pallas_call(
        _memkv_kernel,
        grid=(b,),
        in_specs=[pl.BlockSpec((1, m, d), lambda i: (i, 0, 0)),
                  pl.BlockSpec((1, d), lambda i: (0, 0)),
                  pl.BlockSpec((d, 2 * cw), lambda i: (0, 0)),
                  pl.BlockSpec((1, HEAD_DIM), lambda i: (0, 0))],
        out_specs=[pl.BlockSpec((1, m, cw), lambda i: (i, 0, 0)),
                   pl.BlockSpec((1, m, cw), lambda i: (i, 0, 0))],
        out_shape=[out, out],
        compiler_params=_cparams(("parallel",)),
        name="mem_kv",
    )(mem, g.reshape(1, d), w_kv, kg.reshape(1, HEAD_DIM))


def _merge_kernel(h_ref, oa_ref, ob_ref, qc_ref, km_ref, vm_ref, wg0_ref, wg1_ref, wg2_ref, bg_ref,
                  wba_ref, wbb_ref, wbc_ref, o_ref, oc_ref):
    j = pl.program_id(1)

    @pl.when(j == 0)
    def _():
        scale = HEAD_DIM ** -0.5
        nt = (((1,), (1,)), ((), ()))
        for hh in range(C_HEADS):
            sl = slice(hh * HEAD_DIM, (hh + 1) * HEAD_DIM)
            s = lax.dot_general(qc_ref[:, sl], km_ref[0, :, sl], nt, preferred_element_type=F32) * scale
            m = jnp.max(s, axis=-1, keepdims=True)
            p = jnp.exp(s - m)
            l = jnp.sum(p, axis=-1, keepdims=True)
            o = jnp.dot(p.astype(BF16), vm_ref[0, :, sl], preferred_element_type=F32)
            oc_ref[:, sl] = (o / l).astype(oc_ref.dtype)

    h = h_ref[...]
    merged = None
    for br, (o_br, wg_ref, wb_ref) in enumerate(((oa_ref[...], wg0_ref, wba_ref),
                                                 (ob_ref[...], wg1_ref, wbb_ref),
                                                 (oc_ref[...], wg2_ref, wbc_ref))):
        y = jnp.dot(o_br, wb_ref[...], preferred_element_type=F32)
        g = _sigmoid(jnp.dot(h, wg_ref[...], preferred_element_type=F32) + bg_ref[br:br + 1, :])
        merged = g * y if merged is None else merged + g * y
    o_ref[...] = merged.astype(o_ref.dtype)


def _merge(h, o_a, o_b, proj_a, qc_col, k_mem, v_mem, w_gate, b_gate3, w_branch, seq, tm=1024, tn=512):
    n, d = h.shape
    aw = A_HEADS * HEAD_DIM
    cw = C_HEADS * HEAD_DIM
    n_mem = k_mem.shape[1]
    ncol = d // tn
    tiles_per_seq = seq // tm
    return pl.pallas_call(
        _merge_kernel,
        grid=(n // tm, ncol),
        in_specs=[pl.BlockSpec((tm, d), lambda i, j: (i, 0)),
                  pl.BlockSpec((tm, aw), lambda i, j: (i, 0)),
                  pl.BlockSpec((tm, LRU_WIDTH), lambda i, j: (i, 0)),
                  pl.BlockSpec((tm, cw), lambda i, j: (i, qc_col)),
                  pl.BlockSpec((1, n_mem, cw), lambda i, j: (i // tiles_per_seq, 0, 0)),
                  pl.BlockSpec((1, n_mem, cw), lambda i, j: (i // tiles_per_seq, 0, 0)),
                  pl.BlockSpec((d, tn), lambda i, j: (0, j)),
                  pl.BlockSpec((d, tn), lambda i, j: (0, j + ncol)),
                  pl.BlockSpec((d, tn), lambda i, j: (0, j + 2 * ncol)),
                  pl.BlockSpec((3, tn), lambda i, j: (0, j)),
                  pl.BlockSpec((aw, tn), lambda i, j: (0, j)),
                  pl.BlockSpec((LRU_WIDTH, tn), lambda i, j: (aw // LRU_WIDTH, j)),
                  pl.BlockSpec((cw, tn), lambda i, j: ((aw + LRU_WIDTH) // cw, j))],
        out_specs=pl.BlockSpec((tm, tn), lambda i, j: (i, j)),
        out_shape=jax.ShapeDtypeStruct((n, d), BF16),
        scratch_shapes=[pltpu.VMEM((tm, cw), BF16)],
        compiler_params=_cparams(("parallel", "arbitrary")),
        name="gated_merge",
    )(h, o_a, o_b, proj_a, k_mem, v_mem, w_gate, w_gate, w_gate, b_gate3, w_branch, w_branch, w_branch)


def _oproj_kernel(x_ref, m_ref, w_ref, g_ref, x1_ref, h2t_ref):
    x1 = x_ref[...] + jnp.dot(m_ref[...], w_ref[...], preferred_element_type=F32)
    x1_ref[...] = x1
    ms = jnp.mean(x1 * x1, axis=-1, keepdims=True)
    h2 = x1 * lax.rsqrt(ms + EPS) * g_ref[...]
    h2t_ref[...] = h2.T.astype(h2t_ref.dtype)


def _out_proj(x, merged, w_o, g2, tm=512):
    n, d = x.shape
    return pl.pallas_call(
        _oproj_kernel,
        grid=(n // tm,),
        in_specs=[pl.BlockSpec((tm, d), lambda i: (i, 0)),
                  pl.BlockSpec((tm, d), lambda i: (i, 0)),
                  pl.BlockSpec((d, d), lambda i: (0, 0)),
                  pl.BlockSpec((1, d), lambda i: (0, 0))],
        out_specs=[pl.BlockSpec((tm, d), lambda i: (i, 0)),
                   pl.BlockSpec((d, tm), lambda i: (0, i))],
        out_shape=[jax.ShapeDtypeStruct((n, d), F32), jax.ShapeDtypeStruct((d, n), BF16)],
        compiler_params=_cparams(("parallel",)),
        name="out_proj_norm2",
    )(x, merged, w_o, g2.reshape(1, d))


def _cmpx(v, i, j):
    a, b = v[i], v[j]
    if b is None:
        return
    if a is None:
        v[i], v[j] = b, None
        return
    v[i], v[j] = jnp.maximum(a, b), jnp.minimum(a, b)


def _bitonic_merge_desc(v, lo, n):
    j = n // 2
    while j >= 1:
        for i in range(lo, lo + n):
            if (i - lo) & j == 0:
                _cmpx(v, i, i + j)
        j //= 2


def _bitonic_sort_desc(v):
    n = len(v)
    k = 2
    while k <= n:
        j = k // 2
        while j >= 1:
            for i in range(n):
                l = i ^ j
                if l > i:
                    if i & k == 0 or k == n:
                        _cmpx(v, i, l)
                    else:
                        _cmpx(v, l, i)
            j //= 2
        k *= 2


def _top17_desc(x):
    k = PEER_TOPK
    groups = []
    for g in range(x.shape[0] // k):
        run = [x[g * k + r] for r in range(k)]
        _bitonic_sort_desc(run)
        groups.append(run)
    dropped = None
    while len(groups) > 1:
        merged = []
        for g in range(0, len(groups), 2):
            p, q = groups[g], groups[g + 1]
            hi = [jnp.maximum(p[r], q[k - 1 - r]) for r in range(k)]
            lo = functools.reduce(jnp.maximum, [jnp.minimum(p[r], q[k - 1 - r]) for r in range(k)])
            dropped = lo if dropped is None else jnp.maximum(dropped, lo)
            _bitonic_merge_desc(hi, 0, k)
            merged.append(hi)
        groups = merged
    return groups[0] + [dropped]


def _route_kernel(ht_ref, wq_ref, kb1_ref, kb2_ref, s1_ref, s2h_ref, st_ref,
                  s2_ref, s2c_ref, *, tm):
    nk = N_KEYS * PEER_HEADS
    qt = jnp.dot(wq_ref[...], ht_ref[...], preferred_element_type=F32)
    s1_ref[...] = jnp.dot(kb1_ref[...], qt[0:nk].astype(BF16), preferred_element_type=F32)
    s2_ref[...] = jnp.dot(kb2_ref[...], qt[nk:2 * nk].astype(BF16), preferred_element_type=F32)

    def chunk(tb, carry):
        for sub in range(ROUTE_LANES // LANES):
            lsl = pl.ds(pl.multiple_of(tb * ROUTE_LANES + sub * LANES, LANES), LANES)
            s2c_ref[...] = s2_ref[:, lsl]
            for h in range(PEER_HEADS):
                s2h_ref[h, :, lsl] = s2c_ref[pl.ds(h, N_KEYS, stride=PEER_HEADS), :]
        tsl = pl.ds(pl.multiple_of(tb * ROUTE_LANES, ROUTE_LANES), ROUTE_LANES)
        a = _top17_desc(s1_ref[:, tsl].reshape(N_KEYS, PEER_HEADS, ROUTE_LANES))
        b = _top17_desc(s2_ref[:, tsl].reshape(N_KEYS, PEER_HEADS, ROUTE_LANES))
        nt = PEER_TOPK + 1
        cands = [a[p] + b[q] for p in range(nt) for q in range(nt) if (p + 1) * (q + 1) <= nt]
        order = cands + [None] * (64 - len(cands))
        _bitonic_sort_desc(order)
        thr = 0.5 * (order[PEER_TOPK - 1] + order[PEER_TOPK])
        m = order[0]
        z = functools.reduce(
            jnp.add, [jnp.where(cv >= thr, jnp.exp(cv - m), 0.0) for cv in cands])
        st_ref[0, :, tsl] = thr
        st_ref[1, :, tsl] = a[0]
        st_ref[2, :, tsl] = b[0]
        st_ref[3, :, tsl] = 1.0 / z
        return carry

    lax.fori_loop(0, tm // ROUTE_LANES, chunk, 0)


def _peer_route(h2t, wq_t, kb1, kb2, tm=512):
    d, n = h2t.shape
    nk = N_KEYS * PEER_HEADS
    s_shape = jax.ShapeDtypeStruct((nk, n), F32)
    return pl.pallas_call(
        functools.partial(_route_kernel, tm=tm),
        grid=(n // tm,),
        in_specs=[pl.BlockSpec((d, tm), lambda i: (0, i)),
                  pl.BlockSpec((2 * nk, d), lambda i: (0, 0)),
                  pl.BlockSpec((nk, nk), lambda i: (0, 0)),
                  pl.BlockSpec((nk, nk), lambda i: (0, 0))],
        out_specs=[pl.BlockSpec((nk, tm), lambda i: (0, i)),
                   pl.BlockSpec((PEER_HEADS, N_KEYS, tm), lambda i: (0, 0, i)),
                   pl.BlockSpec((4, PEER_HEADS, tm), lambda i: (0, 0, i))],
        out_shape=[s_shape, jax.ShapeDtypeStruct((PEER_HEADS, N_KEYS, n), F32),
                   jax.ShapeDtypeStruct((4, PEER_HEADS, n), F32)],
        scratch_shapes=[pltpu.VMEM((nk, tm), F32),
                        pltpu.VMEM((nk, LANES), F32)],
        compiler_params=_cparams(("parallel",)),
        name="peer_route",
    )(h2t, wq_t, kb1, kb2)


def _peer_kernel(ht_ref, u_ref, vt_ref, s1_ref, s2h_ref, st_ref, acc_ref,
                 e2_ref, c1_ref, th_ref, z_ref, w_ref, *, tm, te, ne, n_steps):
    s = pl.program_id(0)
    e1 = jnp.minimum(s, n_steps - 1) % ne
    prev = jnp.maximum(s - 1, 0)
    ni = te // N_KEYS
    slot = s % 2

    @pl.when(s == 0)
    def _():
        w_ref[...] = jnp.zeros(w_ref.shape, w_ref.dtype)

    @pl.when(prev % ne == 0)
    def _():
        acc_ref[...] = jnp.zeros(acc_ref.shape, F32)

    @pl.when(e1 == 0)
    def _():
        for h in range(PEER_HEADS):
            e2_ref[h] = jnp.exp(s2h_ref[h] - st_ref[2, h:h + 1, :])
        c1_ref[...] = jnp.exp(s1_ref[...] - st_ref[1][None]) * st_ref[3][None]
        th_ref[...] = st_ref[0][None] - s1_ref[...]

    z_ref[...] = jnp.dot(u_ref[...], ht_ref[...], preferred_element_type=F32)
    acc_ref[...] += jnp.dot(vt_ref[0], w_ref[1 - slot], preferred_element_type=F32)

    for i in range(ni):
        key1 = e1 * ni + i
        for tb in range(tm // LANES):
            tsl = slice(tb * LANES, (tb + 1) * LANES)
            gate = None
            for h in range(PEER_HEADS):
                hit = s2h_ref[h, :, tsl] >= th_ref[key1, h:h + 1, tsl]
                term = jnp.where(hit, e2_ref[h, :, tsl] * c1_ref[key1, h:h + 1, tsl], 0.0)
                gate = term if gate is None else gate + term
            rows = slice(i * N_KEYS, (i + 1) * N_KEYS)
            w_ref[slot, rows, tsl] = (_gelu_sigmoid_form(z_ref[rows, tsl]) * gate).astype(w_ref.dtype)


def _peer_dense(h2t, u, vt, s1, s2h, st, tm=512, te=PEER_TE):
    d, n = h2t.shape
    ne = u.shape[0] // te
    n_steps = (n // tm) * ne

    def cur(s):
        return jnp.minimum(s, n_steps - 1)

    def prev(s):
        return jnp.maximum(s - 1, 0)

    return pl.pallas_call(
        functools.partial(_peer_kernel, tm=tm, te=te, ne=ne, n_steps=n_steps),
        grid=(n_steps + 1,),
        in_specs=[pl.BlockSpec((d, tm), lambda s: (0, cur(s) // ne)),
                  pl.BlockSpec((te, d), lambda s: (cur(s) % ne, 0)),
                  pl.BlockSpec((1, d, te), lambda s: (prev(s) % ne, 0, 0)),
                  pl.BlockSpec((N_KEYS, PEER_HEADS, tm), lambda s: (0, 0, cur(s) // ne)),
                  pl.BlockSpec((PEER_HEADS, N_KEYS, tm), lambda s: (0, 0, cur(s) // ne)),
                  pl.BlockSpec((4, PEER_HEADS, tm), lambda s: (0, 0, cur(s) // ne))],
        out_specs=pl.BlockSpec((d, tm), lambda s: (0, prev(s) // ne)),
        out_shape=jax.ShapeDtypeStruct((d, n), F32),
        scratch_shapes=[pltpu.VMEM((PEER_HEADS, N_KEYS, tm), F32),
                        pltpu.VMEM((N_KEYS, PEER_HEADS, tm), F32),
                        pltpu.VMEM((N_KEYS, PEER_HEADS, tm), F32),
                        pltpu.VMEM((te, tm), F32),
                        pltpu.VMEM((2, te, tm), BF16)],
        compiler_params=_cparams(("arbitrary",)),
        name="peer_dense",
    )(h2t, u, vt, s1, s2h, st)


def _final_kernel(x1_ref, pt_ref, o_ref):
    o_ref[...] = x1_ref[...] + pt_ref[...].T


def _final_add(x1, peer_t, tm=512):
    n, d = x1.shape
    return pl.pallas_call(
        _final_kernel,
        grid=(n // tm,),
        in_specs=[pl.BlockSpec((tm, d), lambda i: (i, 0)),
                  pl.BlockSpec((d, tm), lambda i: (0, i))],
        out_specs=pl.BlockSpec((tm, d), lambda i: (i, 0)),
        out_shape=jax.ShapeDtypeStruct((n, d), F32),
        compiler_params=_cparams(("parallel",)),
        name="final_residual",
    )(x1, peer_t)


def _block_diag(w):
    g, a, b = w.shape
    eye = jnp.eye(g, dtype=w.dtype)
    return jnp.einsum("gab,gh->gahb", w, eye).reshape(g * a, g * b)


def _key_matrix(keys_c):
    h, k, dh = keys_c.shape
    eye = jnp.eye(h, dtype=keys_c.dtype)
    return jnp.einsum("hkd,hg->khgd", keys_c, eye).reshape(k * h, h * dh)


def kernel(x, mem, norm1_g, w_in, attn_q_norm_g, attn_k_norm_g, rel_bias, conv_w, conv_b, lru_wa, lru_ba, lru_wx, lru_bx, lru_lambda, mem_norm_g, w_mem_kv, xq_norm_g, xk_norm_g, w_gate, b_gate, w_branch, w_o, norm2_g, w_peer_q, peer_keys, peer_u, peer_v):
    batch, seq, d = x.shape
    n = batch * seq
    aw = A_HEADS * HEAD_DIM
    cw = C_HEADS * HEAD_DIM
    lw = LRU_WIDTH
    x2 = x.reshape(n, d)

    cuts = np.cumsum([aw, aw, aw, lw, lw])
    wq, wk, wv, wxl, wgl, wqc = jnp.split(w_in, cuts, axis=1)
    w_a = jnp.concatenate([wq, wk, wv, wqc], axis=1).astype(BF16)
    w_b = jnp.concatenate([wxl, wgl], axis=1).astype(BF16)
    gains_a = jnp.concatenate([jnp.tile(attn_q_norm_g * HEAD_DIM ** -0.5, A_HEADS),
                               jnp.tile(attn_k_norm_g, A_HEADS),
                               jnp.ones((aw,), F32), jnp.tile(xq_norm_g, C_HEADS)]).reshape(1, -1)
    wbd = jnp.concatenate([_block_diag(lru_wa), _block_diag(lru_wx)], axis=1).astype(BF16)
    bab = jnp.concatenate([lru_ba, lru_bx]).reshape(1, 2 * lw)
    table = _attn_bias_table(rel_bias)
    hd = PEER_HEADS * N_KEYS
    wq_t = w_peer_q.reshape(d, PEER_HEADS, 2, HEAD_DIM).transpose(2, 1, 3, 0).reshape(2 * hd, d).astype(BF16)
    kb1 = _key_matrix(peer_keys[:, 0]).astype(BF16)
    kb2 = _key_matrix(peer_keys[:, 1]).astype(BF16)
    u_b = peer_u.astype(BF16)
    vt_b = peer_v.reshape(-1, PEER_TE, d).transpose(0, 2, 1).astype(BF16)

    h = _rmsnorm(x2, norm1_g)
    proj_a = _proj(h, w_a, gains_a, (2 * aw, 3 * aw), BF16, "in_proj_qkv")
    xg = _proj(h, w_b, jnp.ones((1, 2 * lw), F32), (0, 2 * lw), F32, "in_proj_lru")
    o_a = _band_attention(proj_a, table, batch, seq, qcol=0, kcol=1, vcol=2)
    o_b = _lru_branch(xg, conv_w, conv_b, wbd, bab, lru_lambda, batch, seq)
    k_mem, v_mem = _mem_kv(mem, mem_norm_g, w_mem_kv.astype(BF16), xk_norm_g)
    merged = _merge(h, o_a, o_b, proj_a, (3 * aw) // cw, k_mem, v_mem, w_gate.astype(BF16),
                    b_gate.reshape(3, d), w_branch.astype(BF16), seq)
    x1, h2t = _out_proj(x2, merged, w_o.astype(BF16), norm2_g)

    s1, s2h, st = _peer_route(h2t, wq_t, kb1, kb2)
    peer_t = _peer_dense(h2t, u_b, vt_b, s1.reshape(N_KEYS, PEER_HEADS, n), s2h, st)
    out = _final_add(x1, peer_t)
    return out.reshape(batch, seq, d)
```
